```python
import jax
import jax.numpy as jnp
from jax import lax
import numpy as np

D_MODEL = 2048
BATCH = 2
SEQ = 8192
DEPTH = 4

GRID_W = 64
CTX_LEN = 256
N_MIXERS = 3
N_MOD = 9
D_FF = 5632
NORM_EPS = 1e-6
CONV_WIDTH = 31
GLA_HEADS = 4
GLA_DK = D_MODEL // (2 * GLA_HEADS)
GLA_DV = D_MODEL // GLA_HEADS
GLA_GATE_RANK = 16
GLA_GATE_NORMALIZER = 16.0
GLA_CHUNK = 64
ROPE_THETA = 10000.0
NA_HEADS = 16
NA_HEAD_DIM = D_MODEL // NA_HEADS
NA_WIN_R = 8
NA_WIN_C = 16

kernel_name = 'hybrid_conv_gla_natten_dit'


def rms_norm(x, g):
    xf = x.astype(jnp.float32)
    y = xf * lax.rsqrt(jnp.mean(xf * xf, axis=-1, keepdims=True) + NORM_EPS)
    return (y * g.astype(jnp.float32)).astype(x.dtype)


def layer_norm(x, g, b):
    xf = x.astype(jnp.float32)
    mu = jnp.mean(xf, axis=-1, keepdims=True)
    var = jnp.mean(jnp.square(xf - mu), axis=-1, keepdims=True)
    y = (xf - mu) * lax.rsqrt(var + NORM_EPS) * g.astype(jnp.float32) + b.astype(jnp.float32)
    return y.astype(x.dtype)


def modulate(x, g, shift, scale):
    return rms_norm(x, g) * (1 + scale) + shift


def sandwich(x, fn, g_pre, g_post, shift, scale, gate, weight):
    return x + weight * gate * rms_norm(fn(modulate(x, g_pre, shift, scale)), g_post)


def swiglu(h, w_gate, w_up, w_down):
    return (jax.nn.silu(h @ w_gate) * (h @ w_up)) @ w_down


def conv_module(h, w_in, b_in, w_dw, b_dw, ln_g, ln_b, w_out, b_out):
    a, g = jnp.split(h @ w_in + b_in, 2, axis=-1)
    u = a * jax.nn.sigmoid(g)
    u = lax.conv_general_dilated(u, w_dw[:, None, :], window_strides=(1,),
                                 padding=[(CONV_WIDTH // 2, CONV_WIDTH // 2)],
                                 dimension_numbers=('NWC', 'WIO', 'NWC'),
                                 feature_group_count=u.shape[-1]) + b_dw
    return jax.nn.silu(layer_norm(u, ln_g, ln_b)) @ w_out + b_out


def rope_1d(t, pos):
    n = t.shape[-1]
    inv = ROPE_THETA ** (-jnp.arange(0, n, 2, dtype=jnp.float32) / n)
    ang = pos.astype(jnp.float32)[:, None] * inv
    cos, sin = jnp.cos(ang)[:, None, :], jnp.sin(ang)[:, None, :]
    t1, t2 = jnp.split(t.astype(jnp.float32), 2, axis=-1)
    return jnp.concatenate([t1 * cos - t2 * sin, t1 * sin + t2 * cos], axis=-1)


def axial_rope(t, rows, cols):
    tr, tc = jnp.split(t, 2, axis=-1)
    return jnp.concatenate([rope_1d(tr, rows), rope_1d(tc, cols)], axis=-1)


def gla_chunk_scan(q, k, v, log_a, s0):
    b, h, l, _ = q.shape
    n = l // GLA_CHUNK

    def chunks(t):
        return jnp.moveaxis(t.reshape(b, h, n, GLA_CHUNK, t.shape[-1]), 2, 0)

    order_mask = jnp.tril(jnp.ones((GLA_CHUNK, GLA_CHUNK), dtype=bool))

    def step(s, inp):
        qc, kc, vc, ac = inp
        cum = jnp.cumsum(ac, axis=-2)
        last = cum[..., -1:, :]
        qe = qc * jnp.exp(cum)
        scores = jnp.einsum('bhtd,bhsd->bhts', qe, kc * jnp.exp(-cum))
        scores = jnp.where(order_mask, scores, 0.0)
        o = jnp.einsum('bhtd,bhdv->bhtv', qe, s) + jnp.einsum('bhts,bhsv->bhtv', scores, vc)
        s = jnp.exp(last)[..., 0, :, None] * s + jnp.einsum('bhsd,bhsv->bhdv', kc * jnp.exp(last - cum), vc)
        return s, o

    s_fin, o = lax.scan(step, s0, (chunks(q), chunks(k), chunks(v), chunks(log_a)))
    return jnp.moveaxis(o, 0, 2).reshape(b, h, l, v.shape[-1]), s_fin


def gla_mixer(h, hc, ctx_out, w_q, w_k, w_v, w_r, b_r, w_a1, w_a2, b_a, norm_g, w_o):
    def project(t, pos):
        bt, lt, _ = t.shape
        q = (t @ w_q).reshape(bt, lt, GLA_HEADS, GLA_DK)
        k = (t @ w_k).reshape(bt, lt, GLA_HEADS, GLA_DK)
        if pos is not None:
            q, k = axial_rope(q, *pos), axial_rope(k, *pos)
        v = t @ w_v

        def log_gate(d):
            z = (t @ w_a1[d]) @ w_a2[d] + b_a[d]
            return jax.nn.log_sigmoid(z.astype(jnp.float32)) / GLA_GATE_NORMALIZER

        def heads(a):
            return jnp.swapaxes(a.reshape(bt, lt, GLA_HEADS, -1), 1, 2).astype(jnp.float32)

        return heads(q) * GLA_DK ** -0.5, heads(k), heads(v), heads(log_gate(0)), heads(log_gate(1))

    def bidirectional(qkva, s_f, s_b):
        q, k, v, la_f, la_b = qkva
        o_f, s_f = gla_chunk_scan(q, k, v, la_f, s_f)
        fl = lambda a: jnp.flip(a, axis=2)
        o_b, s_b = gla_chunk_scan(fl(q), fl(k), fl(v), fl(la_b), s_b)
        return o_f + fl(o_b), s_f, s_b

    def readout(t, o):
        bt, lt, _ = t.shape
        o = rms_norm(jnp.swapaxes(o, 1, 2), norm_g).astype(t.dtype)
        r = jax.nn.silu(t @ w_r + b_r).reshape(bt, lt, GLA_HEADS, GLA_DV)
        return (o * r).reshape(bt, lt, GLA_HEADS * GLA_DV) @ w_o

    s0 = jnp.zeros((hc.shape[0], GLA_HEADS, GLA_DK, GLA_DV), jnp.float32)
    o_c, s_f, s_b = bidirectional(project(hc, None), s0, s0)
    t = jnp.arange(h.shape[1])
    o, _, _ = bidirectional(project(h, (t // GRID_W, t % GRID_W)), s_f, s_b)
    y = readout(h, o)
    yc = readout(hc, o_c) if ctx_out else None
    return y, yc


def na_mixer(h, hc, ctx_out, w_qkv, rpb, w_o):
    b, l, d = h.shape
    lc = hc.shape[1]
    rows = l // GRID_W
    kr = min(NA_WIN_R, rows)
    scale = NA_HEAD_DIM ** -0.5
    q_c, k_c, v_c = (hc @ w_qkv).reshape(b, lc, 3, NA_HEADS, NA_HEAD_DIM).transpose(2, 0, 3, 1, 4)
    qkv = (h @ w_qkv).reshape(b, rows, GRID_W, 3, NA_HEADS, NA_HEAD_DIM).transpose(3, 1, 0, 4, 2, 5)
    q, k, v = qkv[0], qkv[1], qkv[2]
    col = jnp.arange(GRID_W)
    col_idx = jnp.clip(col - NA_WIN_C // 2, 0, GRID_W - NA_WIN_C)[:, None] + jnp.arange(NA_WIN_C)
    col_bias_idx = col_idx - col[:, None] + (NA_WIN_C - 1)

    def row_block(args):
        q_r, r = args
        rs = jnp.clip(r - kr // 2, 0, rows - kr)
        k_b = lax.dynamic_slice_in_dim(k, rs, kr, axis=0)[:, :, :, col_idx]
        v_b = lax.dynamic_slice_in_dim(v, rs, kr, axis=0)[:, :, :, col_idx]
        row_bias_idx = rs + jnp.arange(kr) - r + (NA_WIN_R - 1)
        bias = rpb[:, row_bias_idx[None, :, None], col_bias_idx[:, None, :]]
        s_lat = jnp.einsum('bhqd,rbhqjd->bhqrj', q_r, k_b) * scale + bias
        s_ctx = jnp.einsum('bhqd,bhkd->bhqk', q_r, k_c) * scale
        s = jnp.concatenate([s_lat.reshape(b, NA_HEADS, GRID_W, kr * NA_WIN_C), s_ctx], axis=-1)
        p = jax.nn.softmax(s.astype(jnp.float32), axis=-1).astype(v.dtype)
        p_lat = p[..., :kr * NA_WIN_C].reshape(b, NA_HEADS, GRID_W, kr, NA_WIN_C)
        return (jnp.einsum('bhqrj,rbhqjd->bhqd', p_lat, v_b)
                + jnp.einsum('bhqk,bhkd->bhqd', p[..., kr * NA_WIN_C:], v_c))

    o = lax.map(row_block, (q, jnp.arange(rows)))
    y = o.transpose(1, 0, 3, 2, 4).reshape(b, l, d) @ w_o
    yc = None
    if ctx_out:
        p_c = jax.nn.softmax(jnp.einsum('bhqd,bhkd->bhqk', q_c, k_c).astype(jnp.float32) * scale, axis=-1)
        o_c = jnp.einsum('bhqk,bhkd->bhqd', p_c.astype(v_c.dtype), v_c)
        yc = o_c.transpose(0, 2, 1, 3).reshape(b, lc, d) @ w_o
    return y, yc


def setup_inputs(seed: int = 0) -> dict:
    key = jax.random.key(seed)
    ks = iter(jax.random.split(key, 40))
    nrm = lambda shape, s: jax.random.normal(next(ks), shape, jnp.float32) * s
    D = D_MODEL
    n_conv, n_gla, n_na = (DEPTH + 2) // 3, (DEPTH + 1) // 3, DEPTH // 3
    return {
        'x': nrm((BATCH, SEQ, D), 1.0),
        'c': nrm((BATCH, D), 1.0),
        'ctx': nrm((BATCH, CTX_LEN, D), 1.0),
        'c_ctx': nrm((D,), 1.0),
        'ada_w': nrm((DEPTH, D, N_MOD * D), 0.5 * D ** -0.5),
        'ada_b': nrm((DEPTH, N_MOD * D), 0.02),
        'norm_g': 1.0 + nrm((DEPTH, 6, D), 0.02),
        'ffn_w_gate': nrm((DEPTH, 2, D, D_FF), D ** -0.5),
        'ffn_w_up': nrm((DEPTH, 2, D, D_FF), D ** -0.5),
        'ffn_w_down': nrm((DEPTH, 2, D_FF, D), D_FF ** -0.5),
        'conv_w_in': nrm((n_conv, D, 2 * D), D ** -0.5),
        'conv_b_in': nrm((n_conv, 2 * D), 0.02),
        'conv_w_dw': nrm((n_conv, CONV_WIDTH, D), CONV_WIDTH ** -0.5),
        'conv_b_dw': nrm((n_conv, D), 0.02),
        'conv_ln_g': 1.0 + nrm((n_conv, D), 0.02),
        'conv_ln_b': nrm((n_conv, D), 0.02),
        'conv_w_out': nrm((n_conv, D, D), D ** -0.5),
        'conv_b_out': nrm((n_conv, D), 0.02),
        'gla_w_q': nrm((n_gla, D, GLA_HEADS * GLA_DK), D ** -0.5),
        'gla_w_k': nrm((n_gla, D, GLA_HEADS * GLA_DK), D ** -0.5),
        'gla_w_v': nrm((n_gla, D, GLA_HEADS * GLA_DV), D ** -0.5),
        'gla_w_r': nrm((n_gla, D, GLA_HEADS * GLA_DV), D ** -0.5),
        'gla_b_r': nrm((n_gla, GLA_HEADS * GLA_DV), 0.02),
        'gla_w_a1': nrm((n_gla, 2, D, GLA_GATE_RANK), D ** -0.5),
        'gla_w_a2': nrm((n_gla, 2, GLA_GATE_RANK, GLA_HEADS * GLA_DK), GLA_GATE_RANK ** -0.5),
        'gla_b_a': nrm((n_gla, 2, GLA_HEADS * GLA_DK), 0.02),
        'gla_norm_g': 1.0 + nrm((n_gla, GLA_DV), 0.02),
        'gla_w_o': nrm((n_gla, GLA_HEADS * GLA_DV, D), (GLA_HEADS * GLA_DV) ** -0.5),
        'na_w_qkv': nrm((n_na, D, 3 * D), D ** -0.5),
        'na_rpb': nrm((n_na, NA_HEADS, 2 * NA_WIN_R - 1, 2 * NA_WIN_C - 1), 0.02),
        'na_w_o': nrm((n_na, D, D), D ** -0.5),
    }


def reference(x, c, ctx, c_ctx, ada_w, ada_b, norm_g, ffn_w_gate, ffn_w_up, ffn_w_down,
              conv_w_in, conv_b_in, conv_w_dw, conv_b_dw, conv_ln_g, conv_ln_b, conv_w_out, conv_b_out,
              gla_w_q, gla_w_k, gla_w_v, gla_w_r, gla_b_r, gla_w_a1, gla_w_a2, gla_b_a, gla_norm_g, gla_w_o,
              na_w_qkv, na_rpb, na_w_o):
    reads_ctx = [(i % N_MIXERS) != 0 for i in range(DEPTH)]
    x_lat, x_ctx = x, ctx
    for i in range(DEPTH):
        kind, j = i % N_MIXERS, i // N_MIXERS
        ctx_out = any(reads_ctx[i + 1:])
        ctx_in = ctx_out or reads_ctx[i]
        m = (jax.nn.silu(c) @ ada_w[i] + ada_b[i]).reshape(-1, N_MOD, 1, D_MODEL)
        mc = (jax.nn.silu(c_ctx) @ ada_w[i] + ada_b[i]).reshape(N_MOD, D_MODEL)
        g = norm_g[i]
        ffn1 = lambda t: swiglu(t, ffn_w_gate[i, 0], ffn_w_up[i, 0], ffn_w_down[i, 0])
        ffn2 = lambda t: swiglu(t, ffn_w_gate[i, 1], ffn_w_up[i, 1], ffn_w_down[i, 1])

        x_lat = sandwich(x_lat, ffn1, g[0], g[1], m[:, 0], m[:, 1], m[:, 2], 0.5)
        if ctx_in:
            x_ctx = sandwich(x_ctx, ffn1, g[0], g[1], mc[0], mc[1], mc[2], 0.5)

        h = modulate(x_lat, g[2], m[:, 3], m[:, 4])
        hc = modulate(x_ctx, g[2], mc[3], mc[4]) if ctx_in else None
        if kind == 0:
            conv = lambda t: conv_module(t, conv_w_in[j], conv_b_in[j], conv_w_dw[j], conv_b_dw[j],
                                         conv_ln_g[j], conv_ln_b[j], conv_w_out[j], conv_b_out[j])
            y = conv(h)
            yc = conv(hc) if ctx_out else None
        elif kind == 1:
            y, yc = gla_mixer(h, hc, ctx_out, gla_w_q[j], gla_w_k[j], gla_w_v[j], gla_w_r[j], gla_b_r[j],
                              gla_w_a1[j], gla_w_a2[j], gla_b_a[j], gla_norm_g[j], gla_w_o[j])
        else:
            y, yc = na_mixer(h, hc, ctx_out, na_w_qkv[j], na_rpb[j], na_w_o[j])
        x_lat = x_lat + m[:, 5] * rms_norm(y, g[3])

        x_lat = sandwich(x_lat, ffn2, g[4], g[5], m[:, 6], m[:, 7], m[:, 8], 0.5)
        if ctx_out:
            x_ctx = x_ctx + mc[5] * rms_norm(yc, g[3])
            x_ctx = sandwich(x_ctx, ffn2, g[4], g[5], mc[6], mc[7], mc[8], 0.5)
    return x_lat
```

```python
import functools
import math

import jax
import jax.numpy as jnp
from jax import lax
from jax.experimental import pallas as pl
from jax.experimental.pallas import tpu as pltpu

F32 = jnp.float32
BF16 = jnp.bfloat16

N_MIXERS = 3
N_MOD = 9
NORM_EPS = 1e-6
GRID_W = 64
GLA_HEADS = 4
GLA_GATE_RANK = 16
GLA_GATE_NORMALIZER = 16.0
GLA_CHUNK = 64
ROPE_THETA = 10000.0
NA_HEADS = 16
NA_WIN_R = 8
NA_WIN_C = 16
MASK_BIAS = -1e30

LANE = 128
CONV_HALO = 16
VMEM_LIMIT = 56 * 1024 * 1024


def _cparams(*sem):
    return pltpu.CompilerParams(dimension_semantics=sem, vmem_limit_bytes=VMEM_LIMIT)


def _rms(x, g):
    return x * lax.rsqrt(jnp.mean(x * x, axis=-1, keepdims=True) + NORM_EPS) * g


def _silu(x):
    return x * jax.nn.sigmoid(x)


def _dot(a, b):
    return jnp.dot(a, b, preferred_element_type=F32)


def _dot_nt(a, b):
    return lax.dot_general(a, b, (((1,), (1,)), ((), ())), preferred_element_type=F32)


def _dot_tn(a, b):
    return lax.dot_general(a, b, (((0,), (0,)), ((), ())), preferred_element_type=F32)


def _modulated(x_ref, mod_ref, g_ref, k0, gi):
    m = mod_ref[0]
    y = _rms(x_ref[...], g_ref[gi:gi + 1, :])
    return (y * (1.0 + m[k0 + 1:k0 + 2, :]) + m[k0:k0 + 1, :]).astype(BF16)


def _ada_kernel(c_ref, w_ref, b_ref, o_ref):
    s = _silu(c_ref[...]).astype(BF16)
    o_ref[0] = _dot(s, w_ref[0].astype(BF16)) + b_ref[0]


def _ada(cs, ada_w, ada_b, tn=1024):
    depth, d, n = ada_w.shape
    r = cs.shape[0]
    return pl.pallas_call(
        _ada_kernel,
        grid=(depth, n // tn),
        in_specs=[
            pl.BlockSpec((r, d), lambda i, j: (0, 0)),
            pl.BlockSpec((1, d, tn), lambda i, j: (i, 0, j)),
            pl.BlockSpec((1, 1, tn), lambda i, j: (i, 0, j)),
        ],
        out_specs=pl.BlockSpec((1, r, tn), lambda i, j: (i, 0, j)),
        out_shape=jax.ShapeDtypeStruct((depth, r, n), F32),
        compiler_params=_cparams("parallel", "parallel"),
    )(cs, ada_w, ada_b.reshape(depth, 1, n))


def _ffn_kernel(x_ref, mod_ref, g_ref, wg_ref, wu_ref, wd_ref, o_ref, h_scr, acc_scr, *, k0, gi):
    j = pl.program_id(1)

    @pl.when(j == 0)
    def _():
        h_scr[...] = _modulated(x_ref, mod_ref, g_ref, k0, gi)
        acc_scr[...] = jnp.zeros_like(acc_scr)

    h = h_scr[...]
    gate = _dot(h, wg_ref[...])
    up = _dot(h, wu_ref[...])
    a = (_silu(gate) * up).astype(BF16)
    acc_scr[...] += _dot(a, wd_ref[...])

    @pl.when(j == pl.num_programs(1) - 1)
    def _():
        m = mod_ref[0]
        y = _rms(acc_scr[...], g_ref[gi + 1:gi + 2, :])
        o_ref[...] = x_ref[...] + (0.5 * m[k0 + 2:k0 + 3, :]) * y


def _ffn(x, mod, rows_per_mod, g, wg, wu, wd, *, k0, gi, tm=512, tf=512):
    m_rows, d = x.shape
    f = wg.shape[1]
    tm = min(tm, rows_per_mod)
    mod_idx = lambda i, j: ((i * tm) // rows_per_mod, 0, 0)
    return pl.pallas_call(
        functools.partial(_ffn_kernel, k0=k0, gi=gi),
        grid=(m_rows // tm, f // tf),
        in_specs=[
            pl.BlockSpec((tm, d), lambda i, j: (i, 0)),
            pl.BlockSpec((1, N_MOD, d), mod_idx),
            pl.BlockSpec(g.shape, lambda i, j: (0, 0)),
            pl.BlockSpec((d, tf), lambda i, j: (0, j)),
            pl.BlockSpec((d, tf), lambda i, j: (0, j)),
            pl.BlockSpec((tf, d), lambda i, j: (j, 0)),
        ],
        out_specs=pl.BlockSpec((tm, d), lambda i, j: (i, 0)),
        out_shape=jax.ShapeDtypeStruct((m_rows, d), F32),
        scratch_shapes=[pltpu.VMEM((tm, d), BF16), pltpu.VMEM((tm, d), F32)],
        compiler_params=_cparams("parallel", "arbitrary"),
    )(x, mod, g, wg, wu, wd)


def _proj_plain_kernel(x_ref, mod_ref, g_ref, w_ref, o_ref, h_scr, *, k0, gi):
    @pl.when(pl.program_id(1) == 0)
    def _():
        h_scr[...] = _modulated(x_ref, mod_ref, g_ref, k0, gi)

    o_ref[...] = _dot(h_scr[...], w_ref[...]).astype(o_ref.dtype)


def _proj_glu_kernel(x_ref, mod_ref, g_ref, wa_ref, wg_ref, ba_ref, bg_ref, o_ref, h_scr, *, k0, gi):
    @pl.when(pl.program_id(1) == 0)
    def _():
        h_scr[...] = _modulated(x_ref, mod_ref, g_ref, k0, gi)

    h = h_scr[...]
    a = _dot(h, wa_ref[...]) + ba_ref[...]
    gate = _dot(h, wg_ref[...]) + bg_ref[...]
    o_ref[...] = a * jax.nn.sigmoid(gate)


def _rope_groups(acc, cos_ref, sin_ref, scale, o_ref):
    n_groups = acc.shape[1] // LANE
    for gidx in range(n_groups):
        xg = acc[:, gidx * LANE:(gidx + 1) * LANE]
        t = (gidx % 2) * LANE
        rot = pltpu.roll(xg, LANE // 2, axis=1)
        o_ref[:, gidx * LANE:(gidx + 1) * LANE] = (
            xg * cos_ref[:, t:t + LANE] + rot * sin_ref[:, t:t + LANE]) * scale


def _proj_gla_kernel(x_ref, mod_ref, g_ref, w_ref, b_ref, cos_ref, sin_ref, o_ref, h_scr,
                     *, k0, gi, q_scale, n_qk, n_v, rope):
    j = pl.program_id(1)

    @pl.when(j == 0)
    def _():
        h_scr[...] = _modulated(x_ref, mod_ref, g_ref, k0, gi)

    acc = _dot(h_scr[...], w_ref[...])

    @pl.when(j < 2 * n_qk)
    def _():
        scale = jnp.where(j < n_qk, q_scale, 1.0).astype(F32)
        if rope:
            _rope_groups(acc, cos_ref, sin_ref, scale, o_ref)
        else:
            o_ref[...] = acc * scale

    @pl.when(jnp.logical_and(j >= 2 * n_qk, j < 2 * n_qk + n_v))
    def _():
        o_ref[...] = acc

    @pl.when(j >= 2 * n_qk + n_v)
    def _():
        o_ref[...] = _silu(acc + b_ref[...])


def _gla_gate_kernel(x_ref, mod_ref, g_ref, a1_ref, a2_ref, ba_ref, o_ref, *, k0, gi):
    h = _modulated(x_ref, mod_ref, g_ref, k0, gi)
    low = _dot(h, a1_ref[...]).astype(BF16)
    for d in range(2):
        z = _dot(low, a2_ref[d]) + ba_ref[d]
        log_sig = jnp.minimum(z, 0.0) - jnp.log1p(jnp.exp(-jnp.abs(z)))
        o_ref[d] = log_sig * (1.0 / GLA_GATE_NORMALIZER)


def _proj_common_specs(tm, d, rows_per_mod, g):
    return [
        pl.BlockSpec((tm, d), lambda i, j: (i, 0)),
        pl.BlockSpec((1, N_MOD, d), lambda i, j: ((i * tm) // rows_per_mod, 0, 0)),
        pl.BlockSpec(g.shape, lambda i, j: (0, 0)),
    ]


def _proj_plain(x, mod, rows_per_mod, g, w, *, k0, gi, out_dtype, tm=512, tn=1024):
    m_rows, d = x.shape
    n = w.shape[1]
    tm = min(tm, rows_per_mod)
    return pl.pallas_call(
        functools.partial(_proj_plain_kernel, k0=k0, gi=gi),
        grid=(m_rows // tm, n // tn),
        in_specs=_proj_common_specs(tm, d, rows_per_mod, g) + [pl.BlockSpec((d, tn), lambda i, j: (0, j))],
        out_specs=pl.BlockSpec((tm, tn), lambda i, j: (i, j)),
        out_shape=jax.ShapeDtypeStruct((m_rows, n), out_dtype),
        scratch_shapes=[pltpu.VMEM((tm, d), BF16)],
        compiler_params=_cparams("parallel", "arbitrary"),
    )(x, mod, g, w)


def _proj_glu(x, mod, rows_per_mod, g, w, b, *, k0, gi, tm=512, tn=512):
    m_rows, d = x.shape
    n = w.shape[1] // 2
    nb = n // tn
    tm = min(tm, rows_per_mod)
    b2 = b.reshape(1, 2 * n)
    return pl.pallas_call(
        functools.partial(_proj_glu_kernel, k0=k0, gi=gi),
        grid=(m_rows // tm, nb),
        in_specs=_proj_common_specs(tm, d, rows_per_mod, g) + [
            pl.BlockSpec((d, tn), lambda i, j: (0, j)),
            pl.BlockSpec((d, tn), lambda i, j: (0, j + nb)),
            pl.BlockSpec((1, tn), lambda i, j: (0, j)),
            pl.BlockSpec((1, tn), lambda i, j: (0, j + nb)),
        ],
        out_specs=pl.BlockSpec((tm, tn), lambda i, j: (i, j)),
        out_shape=jax.ShapeDtypeStruct((m_rows, n), F32),
        scratch_shapes=[pltpu.VMEM((tm, d), BF16)],
        compiler_params=_cparams("parallel", "arbitrary"),
    )(x, mod, g, w, w, b2, b2)


def _proj_gla(x, mod, rows_per_mod, g, w_cat, b_r, cos, sin, *, k0, gi, dk_total, dv_total, rope,
              tm=512, tn=1024):
    m_rows, d = x.shape
    n = w_cat.shape[1]
    tm = min(tm, rows_per_mod)
    n_qk, n_v = dk_total // tn, dv_total // tn
    dk = dk_total // GLA_HEADS
    seq_blocks = cos.shape[0] // tm
    tab_spec = pl.BlockSpec((tm, cos.shape[1]), lambda i, j: (i % seq_blocks, 0))
    return pl.pallas_call(
        functools.partial(_proj_gla_kernel, k0=k0, gi=gi, q_scale=dk ** -0.5, n_qk=n_qk, n_v=n_v, rope=rope),
        grid=(m_rows // tm, n // tn),
        in_specs=_proj_common_specs(tm, d, rows_per_mod, g) + [
            pl.BlockSpec((d, tn), lambda i, j: (0, j)),
            pl.BlockSpec((1, tn), lambda i, j: (0, jnp.maximum(j - (2 * n_qk + n_v), 0))),
            tab_spec, tab_spec,
        ],
        out_specs=pl.BlockSpec((tm, tn), lambda i, j: (i, j)),
        out_shape=jax.ShapeDtypeStruct((m_rows, n), F32),
        scratch_shapes=[pltpu.VMEM((tm, d), BF16)],
        compiler_params=_cparams("parallel", "arbitrary"),
    )(x, mod, g, w_cat, b_r.reshape(1, -1), cos, sin)


def _gla_gate(x, mod, rows_per_mod, g, a1_cat, a2_pad, b_a, *, k0, gi, tm=256):
    m_rows, d = x.shape
    n = a2_pad.shape[2]
    tm = min(tm, rows_per_mod)
    return pl.pallas_call(
        functools.partial(_gla_gate_kernel, k0=k0, gi=gi),
        grid=(m_rows // tm,),
        in_specs=[
            pl.BlockSpec((tm, d), lambda i: (i, 0)),
            pl.BlockSpec((1, N_MOD, d), lambda i: ((i * tm) // rows_per_mod, 0, 0)),
            pl.BlockSpec(g.shape, lambda i: (0, 0)),
            pl.BlockSpec(a1_cat.shape, lambda i: (0, 0)),
            pl.BlockSpec(a2_pad.shape, lambda i: (0, 0, 0)),
            pl.BlockSpec((2, 1, n), lambda i: (0, 0, 0)),
        ],
        out_specs=pl.BlockSpec((2, tm, n), lambda i: (0, i, 0)),
        out_shape=jax.ShapeDtypeStruct((2, m_rows, n), F32),
        compiler_params=_cparams("parallel"),
    )(x, mod, g, a1_cat, a2_pad, b_a.reshape(2, 1, n))


def _gla_scan_kernel(q_ref, k_ref, v_ref, la_ref, s0_ref, o_ref, sf_ref, s_scr, *, n_sub, dk, dv):
    d = pl.program_id(1)
    c = pl.program_id(2)
    fwd = d == 0

    @pl.when(c == 0)
    def _():
        s_scr[...] = s0_ref[0, 0]

    ch = GLA_CHUNK
    row = lax.broadcasted_iota(jnp.int32, (ch, ch), 0)
    col = lax.broadcasted_iota(jnp.int32, (ch, ch), 1)
    tri = (row - col) * (1 - 2 * d) >= 0
    tri_f = tri.astype(F32)

    def sub(i, carry):
        r0 = pl.multiple_of(jnp.where(fwd, i, n_sub - 1 - i) * ch, ch)
        for h in range(GLA_HEADS):
            qc = q_ref[pl.ds(r0, ch), h * dk:(h + 1) * dk]
            kc = k_ref[pl.ds(r0, ch), h * dk:(h + 1) * dk]
            vc = v_ref[pl.ds(r0, ch), h * dv:(h + 1) * dv].astype(BF16)
            ac = la_ref[0, pl.ds(r0, ch), h * dk:(h + 1) * dk]
            cum = jnp.dot(tri_f, ac, precision=lax.Precision.HIGHEST, preferred_element_type=F32)
            total = jnp.sum(ac, axis=0, keepdims=True)
            qe = (qc * jnp.exp(cum)).astype(BF16)
            ke = (kc * jnp.exp(-cum)).astype(BF16)
            kd = (kc * jnp.exp(total - cum)).astype(BF16)
            scores = jnp.where(tri, _dot_nt(qe, ke), 0.0).astype(BF16)
            s_old = s_scr[h]
            o_ref[0, pl.ds(r0, ch), h * dv:(h + 1) * dv] = _dot(qe, s_old.astype(BF16)) + _dot(scores, vc)
            decay_t = jnp.transpose(jnp.broadcast_to(jnp.exp(total), (LANE, dk)))
            decay = jnp.concatenate([decay_t] * (dv // LANE), axis=1)
            s_scr[h] = decay * s_old + _dot_tn(kd, vc)
        return carry

    lax.fori_loop(0, n_sub, sub, 0)

    @pl.when(c == pl.num_programs(2) - 1)
    def _():
        sf_ref[0, 0] = s_scr[...]


def _gla_scan(p, la, s0, *, batch, seq, dk, dv, n_sub=4):
    h = GLA_HEADS
    n_sub = min(n_sub, seq // GLA_CHUNK)
    rb = n_sub * GLA_CHUNK
    nblk = seq // rb
    kq, kv = h * dk, h * dv
    blk = lambda b, d, c: b * nblk + c + d * (nblk - 1 - 2 * c)
    o, s_fin = pl.pallas_call(
        functools.partial(_gla_scan_kernel, n_sub=n_sub, dk=dk, dv=dv),
        grid=(batch, 2, nblk),
        in_specs=[
            pl.BlockSpec((rb, kq), lambda b, d, c: (blk(b, d, c), 0)),
            pl.BlockSpec((rb, kq), lambda b, d, c: (blk(b, d, c), 1)),
            pl.BlockSpec((rb, kv), lambda b, d, c: (blk(b, d, c), (2 * kq) // kv)),
            pl.BlockSpec((1, rb, kq), lambda b, d, c: (d, blk(b, d, c), 0)),
            pl.BlockSpec((1, 1, h, dk, dv), lambda b, d, c: (b, d, 0, 0, 0)),
        ],
        out_specs=[
            pl.BlockSpec((1, rb, kv), lambda b, d, c: (d, blk(b, d, c), 0)),
            pl.BlockSpec((1, 1, h, dk, dv), lambda b, d, c: (b, d, 0, 0, 0)),
        ],
        out_shape=[
            jax.ShapeDtypeStruct((2, batch * seq, kv), F32),
            jax.ShapeDtypeStruct((batch, 2, h, dk, dv), F32),
        ],
        scratch_shapes=[pltpu.VMEM((h, dk, dv), F32)],
        compiler_params=_cparams("parallel", "parallel", "arbitrary"),
    )(p, p, p, la, s0)
    return o, s_fin


def _na_kernel(q_ref, k_ref, v_ref, kc_ref, vc_ref, bias_ref, o_ref, *, rows, scale):
    kc = kc_ref[...]
    vc = vc_ref[...]
    w = GRID_W

    def row(r, carry):
        rs = jnp.clip(r - NA_WIN_R // 2, 0, rows - NA_WIN_R)
        q0 = pl.multiple_of(r * w, w)
        k0 = pl.multiple_of(rs * w, w)
        q = q_ref[pl.ds(q0, w), :]
        kw = k_ref[pl.ds(k0, NA_WIN_R * w), :]
        vw = v_ref[pl.ds(k0, NA_WIN_R * w), :]
        s_lat = _dot_nt(q, kw) * scale + bias_ref[0, rs - r + (NA_WIN_R - 1)]
        s_ctx = _dot_nt(q, kc) * scale
        m = jnp.maximum(jnp.max(s_lat, axis=-1, keepdims=True), jnp.max(s_ctx, axis=-1, keepdims=True))
        p_lat = jnp.exp(s_lat - m)
        p_ctx = jnp.exp(s_ctx - m)
        denom = jnp.sum(p_lat, axis=-1, keepdims=True) + jnp.sum(p_ctx, axis=-1, keepdims=True)
        acc = _dot(p_lat.astype(BF16), vw) + _dot(p_ctx.astype(BF16), vc)
        o_ref[pl.ds(q0, w), :] = (acc / denom).astype(o_ref.dtype)
        return carry

    lax.fori_loop(0, rows, row, 0)


def _na_bias_table(rpb):
    w = GRID_W
    col = jnp.arange(w)
    cs = jnp.clip(col - NA_WIN_C // 2, 0, w - NA_WIN_C)
    kc = jnp.arange(w)
    valid = (kc[None, :] >= cs[:, None]) & (kc[None, :] < cs[:, None] + NA_WIN_C)
    rel = jnp.clip(kc[None, :] - col[:, None] + (NA_WIN_C - 1), 0, 2 * NA_WIN_C - 2)
    cb = jnp.where(valid[None, None], rpb[:, :, rel], MASK_BIAS)
    idx = jnp.arange(NA_WIN_R)[:, None] + jnp.arange(NA_WIN_R)[None, :]
    t = cb[:, idx]
    return t.transpose(0, 1, 3, 2, 4).reshape(rpb.shape[0], NA_WIN_R, w, NA_WIN_R * w).astype(F32)


def _na_attention(qkv, qkv_ctx, bias, *, batch, seq, ctx_len, d):
    hd = d // NA_HEADS
    rows = seq // GRID_W
    assert rows >= NA_WIN_R and hd == LANE
    nh = NA_HEADS
    return pl.pallas_call(
        functools.partial(_na_kernel, rows=rows, scale=hd ** -0.5),
        grid=(batch, nh),
        in_specs=[
            pl.BlockSpec((seq, hd), lambda b, h: (b, h)),
            pl.BlockSpec((seq, hd), lambda b, h: (b, nh + h)),
            pl.BlockSpec((seq, hd), lambda b, h: (b, 2 * nh + h)),
            pl.BlockSpec((ctx_len, hd), lambda b, h: (b, nh + h)),
            pl.BlockSpec((ctx_len, hd), lambda b, h: (b, 2 * nh + h)),
            pl.BlockSpec((1,) + bias.shape[1:], lambda b, h: (h, 0, 0, 0)),
        ],
        out_specs=pl.BlockSpec((seq, hd), lambda b, h: (b, h)),
        out_shape=jax.ShapeDtypeStruct((batch * seq, d), BF16),
        compiler_params=_cparams("parallel", "parallel"),
    )(qkv, qkv, qkv, qkv_ctx, qkv_ctx, bias)


def _finish(a, w_ref, b_ref, x_ref, mod_ref, g_ref, o_ref):
    y = _dot(a, w_ref[...])
    if b_ref is not None:
        y = y + b_ref[...]
    m = mod_ref[0]
    o_ref[...] = x_ref[...] + m[5:6, :] * _rms(y, g_ref[3:4, :])


def _out_na_kernel(a_ref, w_ref, x_ref, mod_ref, g_ref, o_ref):
    _finish(a_ref[...], w_ref, None, x_ref, mod_ref, g_ref, o_ref)


def _out_gla_kernel(of_ref, ob_ref, r_ref, ng_ref, w_ref, x_ref, mod_ref, g_ref, o_ref, a_scr, *, dv):
    ng = ng_ref[...]
    for h in range(GLA_HEADS):
        sl = slice(h * dv, (h + 1) * dv)
        o = of_ref[0, :, sl] + ob_ref[0, :, sl]
        a_scr[:, sl] = (_rms(o, ng) * r_ref[:, sl]).astype(BF16)
    _finish(a_scr[...], w_ref, None, x_ref, mod_ref, g_ref, o_ref)


def _out_conv_kernel(prev_ref, cur_ref, next_ref, wdw_ref, bdw_ref, lng_ref, lnb_ref, w_ref, b_ref,
                     x_ref, mod_ref, g_ref, o_ref, ext_scr, *, seq, width):
    tm = cur_ref.shape[0]
    pos = (pl.program_id(0) * tm) % seq
    halo = CONV_HALO
    ext_scr[0:halo, :] = jnp.where(pos > 0, prev_ref[...], 0.0)
    ext_scr[halo:halo + tm, :] = cur_ref[...]
    ext_scr[halo + tm:, :] = jnp.where(pos + tm < seq, next_ref[...], 0.0)
    off = halo - width // 2
    u = bdw_ref[...] + wdw_ref[0:1, :] * ext_scr[off:off + tm, :]
    for k in range(1, width):
        u = u + wdw_ref[k:k + 1, :] * ext_scr[off + k:off + k + tm, :]
    mu = jnp.mean(u, axis=-1, keepdims=True)
    uc = u - mu
    var = jnp.mean(uc * uc, axis=-1, keepdims=True)
    y = uc * lax.rsqrt(var + NORM_EPS) * lng_ref[...] + lnb_ref[...]
    _finish(_silu(y).astype(BF16), w_ref, b_ref, x_ref, mod_ref, g_ref, o_ref)


def _out_tail_specs(tm, d, rows_per_mod, g):
    return [
        pl.BlockSpec((d, d), lambda i: (0, 0)),
        pl.BlockSpec((tm, d), lambda i: (i, 0)),
        pl.BlockSpec((1, N_MOD, d), lambda i: ((i * tm) // rows_per_mod, 0, 0)),
        pl.BlockSpec(g.shape, lambda i: (0, 0)),
    ]


def _out_call(kernel, head_specs, head_args, w, x, mod, rows_per_mod, g, tm, scratch=()):
    m_rows, d = x.shape
    tail = _out_tail_specs(tm, d, rows_per_mod, g)
    return pl.pallas_call(
        kernel,
        grid=(m_rows // tm,),
        in_specs=head_specs + tail,
        out_specs=pl.BlockSpec((tm, d), lambda i: (i, 0)),
        out_shape=jax.ShapeDtypeStruct((m_rows, d), F32),
        scratch_shapes=list(scratch),
        compiler_params=_cparams("parallel"),
    )(*head_args, w, x, mod, g)


def _out_na(a, w, x, mod, rows_per_mod, g, tm=512):
    d = x.shape[1]
    tm = min(tm, rows_per_mod)
    return _out_call(_out_na_kernel, [pl.BlockSpec((tm, d), lambda i: (i, 0))], [a],
                     w, x, mod, rows_per_mod, g, tm)


def _out_gla(o, p, norm_g, w, x, mod, rows_per_mod, g, *, dv, tm=256):
    d = x.shape[1]
    tm = min(tm, rows_per_mod)
    r_blk = (p.shape[1] - d) // d
    specs = [
        pl.BlockSpec((1, tm, d), lambda i: (0, i, 0)),
        pl.BlockSpec((1, tm, d), lambda i: (1, i, 0)),
        pl.BlockSpec((tm, d), lambda i: (i, r_blk)),
        pl.BlockSpec((1, dv), lambda i: (0, 0)),
    ]
    return _out_call(functools.partial(_out_gla_kernel, dv=dv), specs, [o, o, p, norm_g.reshape(1, dv)],
                     w, x, mod, rows_per_mod, g, tm, scratch=[pltpu.VMEM((tm, d), BF16)])


def _out_conv(u, w_dw, b_dw, ln_g, ln_b, w, b, x, mod, rows_per_mod, g, tm=256):
    m_rows, d = x.shape
    tm = min(tm, rows_per_mod)
    width = w_dw.shape[0]
    assert width // 2 < CONV_HALO and tm % CONV_HALO == 0
    hb = tm // CONV_HALO
    last = m_rows // CONV_HALO - 1
    row = lambda v: v.reshape(1, d)
    specs = [
        pl.BlockSpec((CONV_HALO, d), lambda i: (jnp.maximum(i * hb - 1, 0), 0)),
        pl.BlockSpec((tm, d), lambda i: (i, 0)),
        pl.BlockSpec((CONV_HALO, d), lambda i: (jnp.minimum((i + 1) * hb, last), 0)),
        pl.BlockSpec((width, d), lambda i: (0, 0)),
    ] + [pl.BlockSpec((1, d), lambda i: (0, 0))] * 3
    kernel = functools.partial(_out_conv_kernel, seq=rows_per_mod, width=width)
    tail = _out_tail_specs(tm, d, rows_per_mod, g)
    return pl.pallas_call(
        kernel,
        grid=(m_rows // tm,),
        in_specs=specs + [tail[0], pl.BlockSpec((1, d), lambda i: (0, 0))] + tail[1:],
        out_specs=pl.BlockSpec((tm, d), lambda i: (i, 0)),
        out_shape=jax.ShapeDtypeStruct((m_rows, d), F32),
        scratch_shapes=[pltpu.VMEM((tm + 2 * CONV_HALO, d), F32)],
        compiler_params=_cparams("parallel"),
    )(u, u, u, w_dw, row(b_dw), row(ln_g), row(ln_b), w, row(b), x, mod, g)


def _rope_tables(seq, dk):
    n = dk // 2
    inv = ROPE_THETA ** (-jnp.arange(0, n, 2, dtype=F32) / n)
    t = jnp.arange(seq)

    def half(pos):
        ang = pos.astype(F32)[:, None] * inv
        c, s = jnp.cos(ang), jnp.sin(ang)
        return jnp.concatenate([c, c], axis=-1), jnp.concatenate([-s, s], axis=-1)

    cr, sr = half(t // GRID_W)
    cc, sc = half(t % GRID_W)
    return jnp.concatenate([cr, cc], axis=-1), jnp.concatenate([sr, sc], axis=-1)


def kernel(x, c, ctx, c_ctx, ada_w, ada_b, norm_g, ffn_w_gate, ffn_w_up, ffn_w_down, conv_w_in, conv_b_in, conv_w_dw, conv_b_dw, conv_ln_g, conv_ln_b, conv_w_out, conv_b_out, gla_w_q, gla_w_k, gla_w_v, gla_w_r, gla_b_r, gla_w_a1, gla_w_a2, gla_b_a, gla_norm_g, gla_w_o, na_w_qkv, na_rpb, na_w_o):
    batch, seq, d = x.shape
    ctx_len = ctx.shape[1]
    depth = ada_w.shape[0]
    dk_total = gla_w_q.shape[-1]
    dv_total = gla_w_v.shape[-1]
    dk, dv = dk_total // GLA_HEADS, dv_total // GLA_HEADS

    pad = (-(batch + 1)) % 8
    cs = jnp.concatenate([c, c_ctx[None, :], jnp.zeros((pad, d), F32)], axis=0)
    mods = _ada(cs, ada_w, ada_b).reshape(depth, batch + 1 + pad, N_MOD, d)

    wg_all, wu_all, wd_all = (w.astype(BF16) for w in (ffn_w_gate, ffn_w_up, ffn_w_down))

    x_lat = x.reshape(batch * seq, d)
    x_ctx = ctx.reshape(batch * ctx_len, d)
    reads_ctx = [(i % N_MIXERS) != 0 for i in range(depth)]

    for i in range(depth):
        kind, j = i % N_MIXERS, i // N_MIXERS
        ctx_out = any(reads_ctx[i + 1:])
        ctx_in = ctx_out or reads_ctx[i]
        mod_lat = mods[i, :batch]
        mod_ctx = jnp.broadcast_to(mods[i, batch], (batch, N_MOD, d))
        g = norm_g[i]
        lat = dict(mod=mod_lat, rows_per_mod=seq, g=g)
        cx = dict(mod=mod_ctx, rows_per_mod=ctx_len, g=g)

        ffn1 = dict(wg=wg_all[i, 0], wu=wu_all[i, 0], wd=wd_all[i, 0], k0=0, gi=0)
        ffn2 = dict(wg=wg_all[i, 1], wu=wu_all[i, 1], wd=wd_all[i, 1], k0=6, gi=4)

        x_lat = _ffn(x_lat, **lat, **ffn1)
        if ctx_in:
            x_ctx = _ffn(x_ctx, **cx, **ffn1)

        x_ctx_mixed = None
        if kind == 0:
            w_in = conv_w_in[j].astype(BF16)
            w_out = conv_w_out[j].astype(BF16)
            tail = (conv_w_dw[j], conv_b_dw[j], conv_ln_g[j], conv_ln_b[j], w_out, conv_b_out[j])
            u = _proj_glu(x_lat, w=w_in, b=conv_b_in[j], k0=3, gi=2, **lat)
            x_lat = _out_conv(u, *tail, x_lat, mod_lat, seq, g)
            if ctx_out:
                uc = _proj_glu(x_ctx, w=w_in, b=conv_b_in[j], k0=3, gi=2, **cx)
                x_ctx_mixed = _out_conv(uc, *tail, x_ctx, mod_ctx, ctx_len, g)
        elif kind == 1:
            w_cat = jnp.concatenate([gla_w_q[j], gla_w_k[j], gla_w_v[j], gla_w_r[j]], axis=1).astype(BF16)
            a1_cat = jnp.concatenate(
                [gla_w_a1[j, 0], gla_w_a1[j, 1], jnp.zeros((d, LANE - 2 * GLA_GATE_RANK), F32)], axis=1).astype(BF16)
            a2_pad = jnp.zeros((2, LANE, dk_total), F32)
            for dd in range(2):
                a2_pad = a2_pad.at[dd, dd * GLA_GATE_RANK:(dd + 1) * GLA_GATE_RANK].set(gla_w_a2[j, dd])
            a2_pad = a2_pad.astype(BF16)
            w_o = gla_w_o[j].astype(BF16)
            cos, sin = _rope_tables(seq, dk)
            proj = dict(w_cat=w_cat, b_r=gla_b_r[j], k0=3, gi=2, dk_total=dk_total, dv_total=dv_total)
            gate = dict(a1_cat=a1_cat, a2_pad=a2_pad, b_a=gla_b_a[j], k0=3, gi=2)

            p_ctx = _proj_gla(x_ctx, cos=cos, sin=sin, rope=False, **cx, **proj)
            la_ctx = _gla_gate(x_ctx, **cx, **gate)
            s0 = jnp.zeros((batch, 2, GLA_HEADS, dk, dv), F32)
            o_ctx, s_ctx = _gla_scan(p_ctx, la_ctx, s0, batch=batch, seq=ctx_len, dk=dk, dv=dv)

            p_lat = _proj_gla(x_lat, cos=cos, sin=sin, rope=True, **lat, **proj)
            la_lat = _gla_gate(x_lat, **lat, **gate)
            o_lat, _ = _gla_scan(p_lat, la_lat, s_ctx, batch=batch, seq=seq, dk=dk, dv=dv)

            x_lat = _out_gla(o_lat, p_lat, gla_norm_g[j], w_o, x_lat, mod_lat, seq, g, dv=dv)
            if ctx_out:
                x_ctx_mixed = _out_gla(o_ctx, p_ctx, gla_norm_g[j], w_o, x_ctx, mod_ctx, ctx_len, g, dv=dv)
        else:
            assert not ctx_out, "context self-attention output is not needed for this depth"
            w_qkv = na_w_qkv[j].astype(BF16)
            qkv = _proj_plain(x_lat, w=w_qkv, k0=3, gi=2, out_dtype=BF16, **lat)
            qkv_ctx = _proj_plain(x_ctx, w=w_qkv, k0=3, gi=2, out_dtype=BF16, **cx)
            a = _na_attention(qkv, qkv_ctx, _na_bias_table(na_rpb[j]), batch=batch, seq=seq, ctx_len=ctx_len, d=d)
            x_lat = _out_na(a, na_w_o[j].astype(BF16), x_lat, mod_lat, seq, g)

        x_lat = _ffn(x_lat, **lat, **ffn2)
        if ctx_out:
            x_ctx = _ffn(x_ctx_mixed, **cx, **ffn2)

    return x_lat.reshape(batch, seq, d)
```

```python
import functools
import math

import jax
import jax.numpy as jnp
from jax import lax
from jax.experimental import pallas as pl
from jax.experimental.pallas import tpu as pltpu

F32 = jnp.float32
BF16 = jnp.bfloat16

N_MIXERS = 3
N_MOD = 9
NORM_EPS = 1e-6
GRID_W = 64
GLA_HEADS = 4
GLA_GATE_RANK = 16
GLA_GATE_NORMALIZER = 16.0
GLA_CHUNK = 64
ROPE_THETA = 10000.0
NA_HEADS = 16
NA_WIN_R = 8
NA_WIN_C = 16
MASK_BIAS = -1e30
NA_BLOCK_ROWS = 4
NA_BLOCK_UNROLL = 2

LANE = 128
SUBLANE = 8
BF16_ROWS = 16
CONV_HALO = 16
CONV_ROW_BLOCK = 32
CONV_LANE_BLOCK = 256
VMEM_LIMIT = 56 * 1024 * 1024


def _cparams(*sem):
    return pltpu.CompilerParams(dimension_semantics=sem, vmem_limit_bytes=VMEM_LIMIT)


def _rms(x, g):
    return x * lax.rsqrt(jnp.mean(x * x, axis=-1, keepdims=True) + NORM_EPS) * g


def _silu(x):
    return x * jax.nn.sigmoid(x)


def _dot(a, b):
    return jnp.dot(a, b, preferred_element_type=F32)


def _dot_nt(a, b):
    return lax.dot_general(a, b, (((1,), (1,)), ((), ())), preferred_element_type=F32)


def _dot_tn(a, b):
    return lax.dot_general(a, b, (((0,), (0,)), ((), ())), preferred_element_type=F32)


def _modulated(x_ref, mod_ref, g_ref, k0, gi):
    m = mod_ref[0]
    y = _rms(x_ref[...], g_ref[gi:gi + 1, :])
    return (y * (1.0 + m[k0 + 1:k0 + 2, :]) + m[k0:k0 + 1, :]).astype(BF16)


def _ada_kernel(c_ref, w_ref, b_ref, o_ref):
    s = _silu(c_ref[...]).astype(BF16)
    o_ref[0] = _dot(s, w_ref[0].astype(BF16)) + b_ref[0]


def _ada(cs, ada_w, ada_b, tn=1024):
    depth, d, n = ada_w.shape
    r = cs.shape[0]
    return pl.pallas_call(
        _ada_kernel,
        grid=(depth, n // tn),
        in_specs=[
            pl.BlockSpec((r, d), lambda i, j: (0, 0)),
            pl.BlockSpec((1, d, tn), lambda i, j: (i, 0, j)),
            pl.BlockSpec((1, 1, tn), lambda i, j: (i, 0, j)),
        ],
        out_specs=pl.BlockSpec((1, r, tn), lambda i, j: (i, 0, j)),
        out_shape=jax.ShapeDtypeStruct((depth, r, n), F32),
        compiler_params=_cparams("parallel", "parallel"),
    )(cs, ada_w, ada_b.reshape(depth, 1, n))


def _with_tile_input(refs, k0, gi, chunk, body):
    x_ref, xn_ref, mod_ref, modn_ref, g_ref, h0_scr, h1_scr = refs
    i = pl.program_id(0)
    j = pl.program_id(1)
    tm = x_ref.shape[0]

    @pl.when(jnp.logical_and(i == 0, j == 0))
    def _():
        h0_scr[...] = _modulated(x_ref, mod_ref, g_ref, k0, gi)

    r0 = pl.multiple_of(jnp.minimum(j * chunk, tm - chunk), BF16_ROWS)

    def step(h_cur, h_next):
        h_next[pl.ds(r0, chunk), :] = _modulated(xn_ref.at[pl.ds(r0, chunk), :], modn_ref, g_ref, k0, gi)
        body(h_cur[...])

    @pl.when(i % 2 == 0)
    def _():
        step(h0_scr, h1_scr)

    @pl.when(i % 2 == 1)
    def _():
        step(h1_scr, h0_scr)


def _tile_input_specs(tm, d, rows_per_mod, g, n_tiles, n_steps):
    chunk = -(-tm // (n_steps * BF16_ROWS)) * BF16_ROWS
    assert chunk <= tm and tm % BF16_ROWS == 0
    nxt = lambda i: jnp.minimum(i + 1, n_tiles - 1)
    specs = [
        pl.BlockSpec((tm, d), lambda i, j: (i, 0)),
        pl.BlockSpec((tm, d), lambda i, j: (nxt(i), 0)),
        pl.BlockSpec((1, N_MOD, d), lambda i, j: ((i * tm) // rows_per_mod, 0, 0)),
        pl.BlockSpec((1, N_MOD, d), lambda i, j: ((nxt(i) * tm) // rows_per_mod, 0, 0)),
        pl.BlockSpec(g.shape, lambda i, j: (0, 0)),
    ]
    return specs, [pltpu.VMEM((tm, d), BF16), pltpu.VMEM((tm, d), BF16)], chunk


def _ffn_kernel(x_ref, xn_ref, mod_ref, modn_ref, g_ref, wg_ref, wu_ref, wd_ref, o_ref,
                h0_scr, h1_scr, acc_scr, *, k0, gi, chunk):
    j = pl.program_id(1)

    @pl.when(j == 0)
    def _():
        acc_scr[...] = jnp.zeros_like(acc_scr)

    def body(h):
        gate = _dot(h, wg_ref[...])
        up = _dot(h, wu_ref[...])
        a = (_silu(gate) * up).astype(BF16)
        acc_scr[...] += _dot(a, wd_ref[...])

    _with_tile_input((x_ref, xn_ref, mod_ref, modn_ref, g_ref, h0_scr, h1_scr), k0, gi, chunk, body)

    @pl.when(j == pl.num_programs(1) - 1)
    def _():
        m = mod_ref[0]
        y = _rms(acc_scr[...], g_ref[gi + 1:gi + 2, :])
        o_ref[...] = x_ref[...] + (0.5 * m[k0 + 2:k0 + 3, :]) * y


def _ffn(x, mod, rows_per_mod, g, wg, wu, wd, *, layer, which, tm=512, tf=512):
    m_rows, d = x.shape
    f = wg.shape[-1]
    tm = min(tm, rows_per_mod)
    k0, gi = (0, 0) if which == 0 else (6, 4)
    n_tiles, n_steps = m_rows // tm, f // tf
    tile_specs, h_scratch, chunk = _tile_input_specs(tm, d, rows_per_mod, g, n_tiles, n_steps)
    return pl.pallas_call(
        functools.partial(_ffn_kernel, k0=k0, gi=gi, chunk=chunk),
        grid=(n_tiles, n_steps),
        in_specs=tile_specs + [
            pl.BlockSpec((None, None, d, tf), lambda i, j: (layer, which, 0, j)),
            pl.BlockSpec((None, None, d, tf), lambda i, j: (layer, which, 0, j)),
            pl.BlockSpec((None, None, tf, d), lambda i, j: (layer, which, j, 0)),
        ],
        out_specs=pl.BlockSpec((tm, d), lambda i, j: (i, 0)),
        out_shape=jax.ShapeDtypeStruct((m_rows, d), F32),
        scratch_shapes=h_scratch + [pltpu.VMEM((tm, d), F32)],
        compiler_params=_cparams("arbitrary", "arbitrary"),
    )(x, x, mod, mod, g, wg, wu, wd)


def _proj_plain_kernel(x_ref, xn_ref, mod_ref, modn_ref, g_ref, w_ref, o_ref, h0_scr, h1_scr,
                       *, k0, gi, chunk):
    def body(h):
        o_ref[...] = _dot(h, w_ref[...]).astype(o_ref.dtype)

    _with_tile_input((x_ref, xn_ref, mod_ref, modn_ref, g_ref, h0_scr, h1_scr), k0, gi, chunk, body)


def _proj_glu_kernel(x_ref, xn_ref, mod_ref, modn_ref, g_ref, wa_ref, wg_ref, ba_ref, bg_ref, o_ref,
                     h0_scr, h1_scr, *, k0, gi, chunk):
    def body(h):
        a = _dot(h, wa_ref[...]) + ba_ref[...]
        gate = _dot(h, wg_ref[...]) + bg_ref[...]
        o_ref[...] = a * jax.nn.sigmoid(gate)

    _with_tile_input((x_ref, xn_ref, mod_ref, modn_ref, g_ref, h0_scr, h1_scr), k0, gi, chunk, body)


def _rope_groups(acc, cos_ref, sin_ref, scale, o_ref):
    n_groups = acc.shape[1] // LANE
    for gidx in range(n_groups):
        xg = acc[:, gidx * LANE:(gidx + 1) * LANE]
        t = (gidx % 2) * LANE
        rot = pltpu.roll(xg, LANE // 2, axis=1)
        o_ref[:, gidx * LANE:(gidx + 1) * LANE] = (
            xg * cos_ref[:, t:t + LANE] + rot * sin_ref[:, t:t + LANE]) * scale


def _proj_gla_kernel(x_ref, xn_ref, mod_ref, modn_ref, g_ref, w_ref, b_ref, cos_ref, sin_ref, o_ref,
                     h0_scr, h1_scr, *, k0, gi, chunk, q_scale, n_qk, n_v, rope):
    j = pl.program_id(1)

    def body(h):
        acc = _dot(h, w_ref[...])

        @pl.when(j < 2 * n_qk)
        def _():
            scale = jnp.where(j < n_qk, q_scale, 1.0).astype(F32)
            if rope:
                _rope_groups(acc, cos_ref, sin_ref, scale, o_ref)
            else:
                o_ref[...] = acc * scale

        @pl.when(jnp.logical_and(j >= 2 * n_qk, j < 2 * n_qk + n_v))
        def _():
            o_ref[...] = acc

        @pl.when(j >= 2 * n_qk + n_v)
        def _():
            o_ref[...] = _silu(acc + b_ref[...])

    _with_tile_input((x_ref, xn_ref, mod_ref, modn_ref, g_ref, h0_scr, h1_scr), k0, gi, chunk, body)


def _gla_gate_kernel(x_ref, mod_ref, g_ref, a1_ref, a2_ref, ba_ref, o_ref, *, k0, gi):
    h = _modulated(x_ref, mod_ref, g_ref, k0, gi)
    low = _dot(h, a1_ref[...]).astype(BF16)
    for d in range(2):
        z = _dot(low, a2_ref[d]) + ba_ref[d]
        log_sig = jnp.minimum(z, 0.0) - jnp.log1p(jnp.exp(-jnp.abs(z)))
        o_ref[d] = log_sig * (1.0 / GLA_GATE_NORMALIZER)


def _proj_call(kernel_fn, extra_specs, extra_args, x, mod, rows_per_mod, g, n, tn, out_dtype, tm, **static):
    m_rows, d = x.shape
    tm = min(tm, rows_per_mod)
    n_tiles, n_steps = m_rows // tm, n // tn
    tile_specs, h_scratch, chunk = _tile_input_specs(tm, d, rows_per_mod, g, n_tiles, n_steps)
    return pl.pallas_call(
        functools.partial(kernel_fn, chunk=chunk, **static),
        grid=(n_tiles, n_steps),
        in_specs=tile_specs + extra_specs(tm),
        out_specs=pl.BlockSpec((tm, tn), lambda i, j: (i, j)),
        out_shape=jax.ShapeDtypeStruct((m_rows, n), out_dtype),
        scratch_shapes=h_scratch,
        compiler_params=_cparams("arbitrary", "arbitrary"),
    )(x, x, mod, mod, g, *extra_args)


def _proj_plain(x, mod, rows_per_mod, g, w, *, k0, gi, out_dtype, tm=512, tn=1024):
    d, n = w.shape
    specs = lambda tm: [pl.BlockSpec((d, tn), lambda i, j: (0, j))]
    return _proj_call(_proj_plain_kernel, specs, (w,), x, mod, rows_per_mod, g, n, tn, out_dtype, tm,
                      k0=k0, gi=gi)


def _proj_glu(x, mod, rows_per_mod, g, w, b, *, k0, gi, tm=512, tn=512):
    d = w.shape[0]
    n = w.shape[1] // 2
    nb = n // tn
    b2 = b.reshape(1, 2 * n)
    specs = lambda tm: [
        pl.BlockSpec((d, tn), lambda i, j: (0, j)),
        pl.BlockSpec((d, tn), lambda i, j: (0, j + nb)),
        pl.BlockSpec((1, tn), lambda i, j: (0, j)),
        pl.BlockSpec((1, tn), lambda i, j: (0, j + nb)),
    ]
    return _proj_call(_proj_glu_kernel, specs, (w, w, b2, b2), x, mod, rows_per_mod, g, n, tn, F32, tm,
                      k0=k0, gi=gi)


def _proj_gla(x, mod, rows_per_mod, g, w_cat, b_r, cos, sin, *, k0, gi, dk_total, dv_total, rope,
              tm=512, tn=1024):
    d, n = w_cat.shape
    n_qk, n_v = dk_total // tn, dv_total // tn
    dk = dk_total // GLA_HEADS

    def specs(tm):
        seq_blocks = cos.shape[0] // tm
        tab_spec = pl.BlockSpec((tm, cos.shape[1]), lambda i, j: (i % seq_blocks, 0))
        return [
            pl.BlockSpec((d, tn), lambda i, j: (0, j)),
            pl.BlockSpec((1, tn), lambda i, j: (0, jnp.maximum(j - (2 * n_qk + n_v), 0))),
            tab_spec, tab_spec,
        ]

    return _proj_call(_proj_gla_kernel, specs, (w_cat, b_r.reshape(1, -1), cos, sin), x, mod, rows_per_mod, g,
                      n, tn, F32, tm, k0=k0, gi=gi, q_scale=dk ** -0.5, n_qk=n_qk, n_v=n_v, rope=rope)


def _gla_gate(x, mod, rows_per_mod, g, a1_cat, a2_pad, b_a, *, k0, gi, tm=256):
    m_rows, d = x.shape
    n = a2_pad.shape[2]
    tm = min(tm, rows_per_mod)
    return pl.pallas_call(
        functools.partial(_gla_gate_kernel, k0=k0, gi=gi),
        grid=(m_rows // tm,),
        in_specs=[
            pl.BlockSpec((tm, d), lambda i: (i, 0)),
            pl.BlockSpec((1, N_MOD, d), lambda i: ((i * tm) // rows_per_mod, 0, 0)),
            pl.BlockSpec(g.shape, lambda i: (0, 0)),
            pl.BlockSpec(a1_cat.shape, lambda i: (0, 0)),
            pl.BlockSpec(a2_pad.shape, lambda i: (0, 0, 0)),
            pl.BlockSpec((2, 1, n), lambda i: (0, 0, 0)),
        ],
        out_specs=pl.BlockSpec((2, tm, n), lambda i: (0, i, 0)),
        out_shape=jax.ShapeDtypeStruct((2, m_rows, n), F32),
        compiler_params=_cparams("parallel"),
    )(x, mod, g, a1_cat, a2_pad, b_a.reshape(2, 1, n))


def _gla_scan_kernel(q_ref, k_ref, v_ref, la_ref, s0_ref, o_ref, sf_ref, s_scr, *, n_sub, dk, dv):
    d = pl.program_id(1)
    c = pl.program_id(2)
    fwd = d == 0

    @pl.when(c == 0)
    def _():
        s_scr[...] = s0_ref[0, 0]

    ch = GLA_CHUNK
    row = lax.broadcasted_iota(jnp.int32, (ch, ch), 0)
    col = lax.broadcasted_iota(jnp.int32, (ch, ch), 1)
    tri = (row - col) * (1 - 2 * d) >= 0
    tri_f = tri.astype(F32)

    def sub(i, carry):
        r0 = pl.multiple_of(jnp.where(fwd, i, n_sub - 1 - i) * ch, ch)
        for h in range(GLA_HEADS):
            qc = q_ref[pl.ds(r0, ch), h * dk:(h + 1) * dk]
            kc = k_ref[pl.ds(r0, ch), h * dk:(h + 1) * dk]
            vc = v_ref[pl.ds(r0, ch), h * dv:(h + 1) * dv].astype(BF16)
            ac = la_ref[0, pl.ds(r0, ch), h * dk:(h + 1) * dk]
            cum = jnp.dot(tri_f, ac, precision=lax.Precision.HIGHEST, preferred_element_type=F32)
            total = jnp.sum(ac, axis=0, keepdims=True)
            qe = (qc * jnp.exp(cum)).astype(BF16)
            ke = (kc * jnp.exp(-cum)).astype(BF16)
            kd = (kc * jnp.exp(total - cum)).astype(BF16)
            scores = jnp.where(tri, _dot_nt(qe, ke), 0.0).astype(BF16)
            s_old = s_scr[h]
            o_ref[0, pl.ds(r0, ch), h * dv:(h + 1) * dv] = _dot(qe, s_old.astype(BF16)) + _dot(scores, vc)
            decay_t = jnp.transpose(jnp.broadcast_to(jnp.exp(total), (LANE, dk)))
            decay = jnp.concatenate([decay_t] * (dv // LANE), axis=1)
            s_scr[h] = decay * s_old + _dot_tn(kd, vc)
        return carry

    lax.fori_loop(0, n_sub, sub, 0)

    @pl.when(c == pl.num_programs(2) - 1)
    def _():
        sf_ref[0, 0] = s_scr[...]


def _gla_scan(p, la, s0, *, batch, seq, dk, dv, n_sub=4):
    h = GLA_HEADS
    n_sub = min(n_sub, seq // GLA_CHUNK)
    rb = n_sub * GLA_CHUNK
    nblk = seq // rb
    kq, kv = h * dk, h * dv
    blk = lambda b, d, c: b * nblk + c + d * (nblk - 1 - 2 * c)
    o, s_fin = pl.pallas_call(
        functools.partial(_gla_scan_kernel, n_sub=n_sub, dk=dk, dv=dv),
        grid=(batch, 2, nblk),
        in_specs=[
            pl.BlockSpec((rb, kq), lambda b, d, c: (blk(b, d, c), 0)),
            pl.BlockSpec((rb, kq), lambda b, d, c: (blk(b, d, c), 1)),
            pl.BlockSpec((rb, kv), lambda b, d, c: (blk(b, d, c), (2 * kq) // kv)),
            pl.BlockSpec((1, rb, kq), lambda b, d, c: (d, blk(b, d, c), 0)),
            pl.BlockSpec((1, 1, h, dk, dv), lambda b, d, c: (b, d, 0, 0, 0)),
        ],
        out_specs=[
            pl.BlockSpec((1, rb, kv), lambda b, d, c: (d, blk(b, d, c), 0)),
            pl.BlockSpec((1, 1, h, dk, dv), lambda b, d, c: (b, d, 0, 0, 0)),
        ],
        out_shape=[
            jax.ShapeDtypeStruct((2, batch * seq, kv), F32),
            jax.ShapeDtypeStruct((batch, 2, h, dk, dv), F32),
        ],
        scratch_shapes=[pltpu.VMEM((h, dk, dv), F32)],
        compiler_params=_cparams("parallel", "parallel", "arbitrary"),
    )(p, p, p, la, s0)
    return o, s_fin


def _na_kernel(q_ref, k_ref, v_ref, kc_ref, vc_ref, bias_ref, o_ref, *, rows, scale, plan):
    kc = kc_ref[...]
    vc = vc_ref[...]
    w = GRID_W
    n_q = NA_BLOCK_ROWS * w
    n_k = (NA_BLOCK_ROWS + NA_WIN_R) * w
    common, exceptions = plan

    def block(b, carry):
        r0 = b * NA_BLOCK_ROWS
        ws = jnp.clip(r0 - NA_WIN_R // 2, 0, rows - (NA_BLOCK_ROWS + NA_WIN_R))
        pid = common
        for blk, p in exceptions:
            pid = jnp.where(b == blk, p, pid)
        q0 = pl.multiple_of(r0 * w, w)
        k0 = pl.multiple_of(ws * w, w)
        q = q_ref[pl.ds(q0, n_q), :]
        kw = k_ref[pl.ds(k0, n_k), :]
        vw = v_ref[pl.ds(k0, n_k), :]
        s_lat = _dot_nt(q, kw) * scale + bias_ref[0, pid]
        s_ctx = _dot_nt(q, kc) * scale
        m = jnp.maximum(jnp.max(s_lat, axis=-1, keepdims=True), jnp.max(s_ctx, axis=-1, keepdims=True))
        p_lat = jnp.exp(s_lat - m)
        p_ctx = jnp.exp(s_ctx - m)
        denom = jnp.sum(p_lat, axis=-1, keepdims=True) + jnp.sum(p_ctx, axis=-1, keepdims=True)
        acc = _dot(p_lat.astype(BF16), vw) + _dot(p_ctx.astype(BF16), vc)
        o_ref[pl.ds(q0, n_q), :] = (acc / denom).astype(o_ref.dtype)
        return carry

    lax.fori_loop(0, rows // NA_BLOCK_ROWS, block, 0, unroll=NA_BLOCK_UNROLL)


def _na_plan(rows):
    kw_rows = NA_BLOCK_ROWS + NA_WIN_R
    patterns, ids = [], []
    for b in range(rows // NA_BLOCK_ROWS):
        r0 = b * NA_BLOCK_ROWS
        ws = min(max(r0 - NA_WIN_R // 2, 0), rows - kw_rows)
        key = []
        for rho in range(NA_BLOCK_ROWS):
            r = r0 + rho
            rs = min(max(r - NA_WIN_R // 2, 0), rows - NA_WIN_R)
            assert ws <= rs and rs + NA_WIN_R <= ws + kw_rows
            key.append((rs - ws, ws - r + NA_WIN_R - 1))
        key = tuple(key)
        if key not in patterns:
            patterns.append(key)
        ids.append(patterns.index(key))
    common = max(set(ids), key=ids.count)
    exceptions = tuple((b, p) for b, p in enumerate(ids) if p != common)
    dr = [[[min(max(base + om, 0), 2 * NA_WIN_R - 2) for om in range(kw_rows)] for (_, base) in pat]
          for pat in patterns]
    valid = [[[first <= om < first + NA_WIN_R for om in range(kw_rows)] for (first, _) in pat]
             for pat in patterns]
    return dr, valid, (common, exceptions)


def _na_bias_table(rpb, rows):
    w = GRID_W
    dr, valid_rows, plan = _na_plan(rows)
    col = jnp.arange(w)
    cs = jnp.clip(col - NA_WIN_C // 2, 0, w - NA_WIN_C)
    valid_cols = (col[None, :] >= cs[:, None]) & (col[None, :] < cs[:, None] + NA_WIN_C)
    rel = jnp.clip(col[None, :] - col[:, None] + (NA_WIN_C - 1), 0, 2 * NA_WIN_C - 2)
    cb = jnp.where(valid_cols[None, None], rpb[:, :, rel], MASK_BIAS)
    t = cb[:, jnp.asarray(dr)]
    t = jnp.where(jnp.asarray(valid_rows)[None, :, :, :, None, None], t, MASK_BIAS)
    n_pat = len(dr)
    t = t.transpose(0, 1, 2, 4, 3, 5).reshape(
        rpb.shape[0], n_pat, NA_BLOCK_ROWS * w, (NA_BLOCK_ROWS + NA_WIN_R) * w)
    return t.astype(F32), plan


def _na_attention(qkv, qkv_ctx, bias, plan, *, batch, seq, ctx_len, d):
    hd = d // NA_HEADS
    rows = seq // GRID_W
    assert rows >= NA_BLOCK_ROWS + NA_WIN_R and rows % NA_BLOCK_ROWS == 0 and hd == LANE
    nh = NA_HEADS
    return pl.pallas_call(
        functools.partial(_na_kernel, rows=rows, scale=hd ** -0.5, plan=plan),
        grid=(batch, nh),
        in_specs=[
            pl.BlockSpec((seq, hd), lambda b, h: (b, h)),
            pl.BlockSpec((seq, hd), lambda b, h: (b, nh + h)),
            pl.BlockSpec((seq, hd), lambda b, h: (b, 2 * nh + h)),
            pl.BlockSpec((ctx_len, hd), lambda b, h: (b, nh + h)),
            pl.BlockSpec((ctx_len, hd), lambda b, h: (b, 2 * nh + h)),
            pl.BlockSpec((1,) + bias.shape[1:], lambda b, h: (h, 0, 0, 0)),
        ],
        out_specs=pl.BlockSpec((seq, hd), lambda b, h: (b, h)),
        out_shape=jax.ShapeDtypeStruct((batch * seq, d), BF16),
        compiler_params=_cparams("parallel", "parallel"),
    )(qkv, qkv, qkv, qkv_ctx, qkv_ctx, bias)


def _finish(a, w_ref, b_ref, x_ref, mod_ref, g_ref, o_ref):
    y = _dot(a, w_ref[...])
    if b_ref is not None:
        y = y + b_ref[...]
    m = mod_ref[0]
    o_ref[...] = x_ref[...] + m[5:6, :] * _rms(y, g_ref[3:4, :])


def _out_na_kernel(a_ref, w_ref, x_ref, mod_ref, g_ref, o_ref):
    _finish(a_ref[...], w_ref, None, x_ref, mod_ref, g_ref, o_ref)


def _out_gla_kernel(of_ref, ob_ref, r_ref, ng_ref, w_ref, x_ref, mod_ref, g_ref, o_ref, a_scr, *, dv):
    ng = ng_ref[...]
    for h in range(GLA_HEADS):
        sl = slice(h * dv, (h + 1) * dv)
        o = of_ref[0, :, sl] + ob_ref[0, :, sl]
        a_scr[:, sl] = (_rms(o, ng) * r_ref[:, sl]).astype(BF16)
    _finish(a_scr[...], w_ref, None, x_ref, mod_ref, g_ref, o_ref)


def _out_conv_kernel(prev_ref, cur_ref, next_ref, wdw_ref, bdw_ref, lng_ref, lnb_ref, w_ref, b_ref,
                     x_ref, mod_ref, g_ref, o_ref, ext_scr, u_scr, *, seq, width):
    tm = cur_ref.shape[0]
    pos = (pl.program_id(0) * tm) % seq
    halo = CONV_HALO
    ext_scr[0:halo, :] = jnp.where(pos > 0, prev_ref[...], 0.0)
    ext_scr[halo:halo + tm, :] = cur_ref[...]
    ext_scr[halo + tm:, :] = jnp.where(pos + tm < seq, next_ref[...], 0.0)
    off = halo - width // 2
    d = cur_ref.shape[1]
    rb, ls = CONV_ROW_BLOCK, CONV_LANE_BLOCK
    n_q = (off + width - 1) // SUBLANE + 1

    def rows(i, carry):
        r0 = pl.multiple_of(i * rb, rb)
        for c0 in range(0, d, ls):
            v = ext_scr[pl.ds(r0, rb + 2 * halo), c0:c0 + ls]
            acc = jnp.broadcast_to(bdw_ref[:, c0:c0 + ls], (rb, ls))
            for s in range(SUBLANE):
                vs = v[s:s + rb + SUBLANE * (n_q - 1)]
                for q in range(n_q):
                    k = SUBLANE * q + s - off
                    if 0 <= k < width:
                        wk = jnp.concatenate([wdw_ref[k, :, c0:c0 + ls]] * (rb // SUBLANE), axis=0)
                        acc = acc + wk * vs[SUBLANE * q:SUBLANE * q + rb]
            u_scr[pl.ds(r0, rb), c0:c0 + ls] = acc
        return carry

    lax.fori_loop(0, tm // rb, rows, 0)
    u = u_scr[...]
    mu = jnp.mean(u, axis=-1, keepdims=True)
    uc = u - mu
    var = jnp.mean(uc * uc, axis=-1, keepdims=True)
    y = uc * lax.rsqrt(var + NORM_EPS) * lng_ref[...] + lnb_ref[...]
    _finish(_silu(y).astype(BF16), w_ref, b_ref, x_ref, mod_ref, g_ref, o_ref)


def _out_tail_specs(tm, d, rows_per_mod, g):
    return [
        pl.BlockSpec((d, d), lambda i: (0, 0)),
        pl.BlockSpec((tm, d), lambda i: (i, 0)),
        pl.BlockSpec((1, N_MOD, d), lambda i: ((i * tm) // rows_per_mod, 0, 0)),
        pl.BlockSpec(g.shape, lambda i: (0, 0)),
    ]


def _out_call(kernel, head_specs, head_args, w, x, mod, rows_per_mod, g, tm, scratch=()):
    m_rows, d = x.shape
    tail = _out_tail_specs(tm, d, rows_per_mod, g)
    return pl.pallas_call(
        kernel,
        grid=(m_rows // tm,),
        in_specs=head_specs + tail,
        out_specs=pl.BlockSpec((tm, d), lambda i: (i, 0)),
        out_shape=jax.ShapeDtypeStruct((m_rows, d), F32),
        scratch_shapes=list(scratch),
        compiler_params=_cparams("parallel"),
    )(*head_args, w, x, mod, g)


def _out_na(a, w, x, mod, rows_per_mod, g, tm=512):
    d = x.shape[1]
    tm = min(tm, rows_per_mod)
    return _out_call(_out_na_kernel, [pl.BlockSpec((tm, d), lambda i: (i, 0))], [a],
                     w, x, mod, rows_per_mod, g, tm)


def _out_gla(o, p, norm_g, w, x, mod, rows_per_mod, g, *, dv, tm=256):
    d = x.shape[1]
    tm = min(tm, rows_per_mod)
    r_blk = (p.shape[1] - d) // d
    specs = [
        pl.BlockSpec((1, tm, d), lambda i: (0, i, 0)),
        pl.BlockSpec((1, tm, d), lambda i: (1, i, 0)),
        pl.BlockSpec((tm, d), lambda i: (i, r_blk)),
        pl.BlockSpec((1, dv), lambda i: (0, 0)),
    ]
    return _out_call(functools.partial(_out_gla_kernel, dv=dv), specs, [o, o, p, norm_g.reshape(1, dv)],
                     w, x, mod, rows_per_mod, g, tm, scratch=[pltpu.VMEM((tm, d), BF16)])


def _out_conv(u, w_dw, b_dw, ln_g, ln_b, w, b, x, mod, rows_per_mod, g, tm=256):
    m_rows, d = x.shape
    tm = min(tm, rows_per_mod)
    width = w_dw.shape[0]
    assert width // 2 < CONV_HALO and tm % CONV_HALO == 0 and tm % CONV_ROW_BLOCK == 0
    hb = tm // CONV_HALO
    last = m_rows // CONV_HALO - 1
    row = lambda v: v.reshape(1, d)
    specs = [
        pl.BlockSpec((CONV_HALO, d), lambda i: (jnp.maximum(i * hb - 1, 0), 0)),
        pl.BlockSpec((tm, d), lambda i: (i, 0)),
        pl.BlockSpec((CONV_HALO, d), lambda i: (jnp.minimum((i + 1) * hb, last), 0)),
        pl.BlockSpec((width, SUBLANE, d), lambda i: (0, 0, 0)),
    ] + [pl.BlockSpec((1, d), lambda i: (0, 0))] * 3
    kernel = functools.partial(_out_conv_kernel, seq=rows_per_mod, width=width)
    tail = _out_tail_specs(tm, d, rows_per_mod, g)
    return pl.pallas_call(
        kernel,
        grid=(m_rows // tm,),
        in_specs=specs + [tail[0], pl.BlockSpec((1, d), lambda i: (0, 0))] + tail[1:],
        out_specs=pl.BlockSpec((tm, d), lambda i: (i, 0)),
        out_shape=jax.ShapeDtypeStruct((m_rows, d), F32),
        scratch_shapes=[pltpu.VMEM((tm + 2 * CONV_HALO, d), F32), pltpu.VMEM((tm, d), F32)],
        compiler_params=_cparams("parallel"),
    )(u, u, u, jnp.broadcast_to(w_dw[:, None, :], (width, SUBLANE, d)),
      row(b_dw), row(ln_g), row(ln_b), w, row(b), x, mod, g)


def _rope_tables(seq, dk):
    n = dk // 2
    inv = ROPE_THETA ** (-jnp.arange(0, n, 2, dtype=F32) / n)
    t = jnp.arange(seq)

    def half(pos):
        ang = pos.astype(F32)[:, None] * inv
        c, s = jnp.cos(ang), jnp.sin(ang)
        return jnp.concatenate([c, c], axis=-1), jnp.concatenate([-s, s], axis=-1)

    cr, sr = half(t // GRID_W)
    cc, sc = half(t % GRID_W)
    return jnp.concatenate([cr, cc], axis=-1), jnp.concatenate([sr, sc], axis=-1)


def kernel(x, c, ctx, c_ctx, ada_w, ada_b, norm_g, ffn_w_gate, ffn_w_up, ffn_w_down, conv_w_in, conv_b_in, conv_w_dw, conv_b_dw, conv_ln_g, conv_ln_b, conv_w_out, conv_b_out, gla_w_q, gla_w_k, gla_w_v, gla_w_r, gla_b_r, gla_w_a1, gla_w_a2, gla_b_a, gla_norm_g, gla_w_o, na_w_qkv, na_rpb, na_w_o):
    batch, seq, d = x.shape
    ctx_len = ctx.shape[1]
    depth = ada_w.shape[0]
    dk_total = gla_w_q.shape[-1]
    dv_total = gla_w_v.shape[-1]
    dk, dv = dk_total // GLA_HEADS, dv_total // GLA_HEADS

    pad = (-(batch + 1)) % 8
    cs = jnp.concatenate([c, c_ctx[None, :], jnp.zeros((pad, d), F32)], axis=0)
    mods = _ada(cs, ada_w, ada_b).reshape(depth, batch + 1 + pad, N_MOD, d)

    wg_all, wu_all, wd_all = (w.astype(BF16) for w in (ffn_w_gate, ffn_w_up, ffn_w_down))

    x_lat = x.reshape(batch * seq, d)
    x_ctx = ctx.reshape(batch * ctx_len, d)
    reads_ctx = [(i % N_MIXERS) != 0 for i in range(depth)]

    for i in range(depth):
        kind, j = i % N_MIXERS, i // N_MIXERS
        ctx_out = any(reads_ctx[i + 1:])
        ctx_in = ctx_out or reads_ctx[i]
        mod_lat = mods[i, :batch]
        mod_ctx = jnp.broadcast_to(mods[i, batch], (batch, N_MOD, d))
        g = norm_g[i]
        lat = dict(mod=mod_lat, rows_per_mod=seq, g=g)
        cx = dict(mod=mod_ctx, rows_per_mod=ctx_len, g=g)

        ffn1 = dict(wg=wg_all, wu=wu_all, wd=wd_all, layer=i, which=0)
        ffn2 = dict(wg=wg_all, wu=wu_all, wd=wd_all, layer=i, which=1)

        x_lat = _ffn(x_lat, **lat, **ffn1)
        if ctx_in:
            x_ctx = _ffn(x_ctx, **cx, **ffn1)

        x_ctx_mixed = None
        if kind == 0:
            w_in = conv_w_in[j].astype(BF16)
            w_out = conv_w_out[j].astype(BF16)
            tail = (conv_w_dw[j], conv_b_dw[j], conv_ln_g[j], conv_ln_b[j], w_out, conv_b_out[j])
            u = _proj_glu(x_lat, w=w_in, b=conv_b_in[j], k0=3, gi=2, **lat)
            x_lat = _out_conv(u, *tail, x_lat, mod_lat, seq, g)
            if ctx_out:
                uc = _proj_glu(x_ctx, w=w_in, b=conv_b_in[j], k0=3, gi=2, **cx)
                x_ctx_mixed = _out_conv(uc, *tail, x_ctx, mod_ctx, ctx_len, g)
        elif kind == 1:
            w_cat = jnp.concatenate([gla_w_q[j], gla_w_k[j], gla_w_v[j], gla_w_r[j]], axis=1).astype(BF16)
            a1_cat = jnp.concatenate(
                [gla_w_a1[j, 0], gla_w_a1[j, 1], jnp.zeros((d, LANE - 2 * GLA_GATE_RANK), F32)], axis=1).astype(BF16)
            a2_pad = jnp.zeros((2, LANE, dk_total), F32)
            for dd in range(2):
                a2_pad = a2_pad.at[dd, dd * GLA_GATE_RANK:(dd + 1) * GLA_GATE_RANK].set(gla_w_a2[j, dd])
            a2_pad = a2_pad.astype(BF16)
            w_o = gla_w_o[j].astype(BF16)
            cos, sin = _rope_tables(seq, dk)
            proj = dict(w_cat=w_cat, b_r=gla_b_r[j], k0=3, gi=2, dk_total=dk_total, dv_total=dv_total)
            gate = dict(a1_cat=a1_cat, a2_pad=a2_pad, b_a=gla_b_a[j], k0=3, gi=2)

            p_ctx = _proj_gla(x_ctx, cos=cos, sin=sin, rope=False, **cx, **proj)
            la_ctx = _gla_gate(x_ctx, **cx, **gate)
            s0 = jnp.zeros((batch, 2, GLA_HEADS, dk, dv), F32)
            o_ctx, s_ctx = _gla_scan(p_ctx, la_ctx, s0, batch=batch, seq=ctx_len, dk=dk, dv=dv)

            p_lat = _proj_gla(x_lat, cos=cos, sin=sin, rope=True, **lat, **proj)
            la_lat = _gla_gate(x_lat, **lat, **gate)
            o_lat, _ = _gla_scan(p_lat, la_lat, s_ctx, batch=batch, seq=seq, dk=dk, dv=dv)

            x_lat = _out_gla(o_lat, p_lat, gla_norm_g[j], w_o, x_lat, mod_lat, seq, g, dv=dv)
            if ctx_out:
                x_ctx_mixed = _out_gla(o_ctx, p_ctx, gla_norm_g[j], w_o, x_ctx, mod_ctx, ctx_len, g, dv=dv)
        else:
            assert not ctx_out, "context self-attention output is not needed for this depth"
            w_qkv = na_w_qkv[j].astype(BF16)
            qkv = _proj_plain(x_lat, w=w_qkv, k0=3, gi=2, out_dtype=BF16, **lat)
            qkv_ctx = _proj_plain(x_ctx, w=w_qkv, k0=3, gi=2, out_dtype=BF16, **cx)
            bias, plan = _na_bias_table(na_rpb[j], seq // GRID_W)
            a = _na_attention(qkv, qkv_ctx, bias, plan, batch=batch, seq=seq, ctx_len=ctx_len, d=d)
            x_lat = _out_na(a, na_w_o[j].astype(BF16), x_lat, mod_lat, seq, g)

        x_lat = _ffn(x_lat, **lat, **ffn2)
        if ctx_out:
            x_ctx = _ffn(x_ctx_mixed, **cx, **ffn2)

    return x_lat.reshape(batch, seq, d)
```

```python
import functools
import math

import jax
import jax.numpy as jnp
from jax import lax
from jax.experimental import pallas as pl
from jax.experimental.pallas import tpu as pltpu

F32 = jnp.float32
BF16 = jnp.bfloat16

N_MIXERS = 3
N_MOD = 9
NORM_EPS = 1e-6
GRID_W = 64
GLA_HEADS = 4
GLA_GATE_RANK = 16
GLA_GATE_NORMALIZER = 16.0
GLA_CHUNK = 64
ROPE_THETA = 10000.0
NA_HEADS = 16
NA_WIN_R = 8
NA_WIN_C = 16
MASK_BIAS = -1e30
NA_BLOCK_ROWS = 4
NA_BLOCK_UNROLL = 2

LANE = 128
SUBLANE = 8
BF16_ROWS = 16
CONV_HALO = 16
CONV_ROW_BLOCK = 64
CONV_LANE_BLOCK = 128
GLU_COL_CHUNK = 512
VMEM_LIMIT = 56 * 1024 * 1024


def _cparams(*sem):
    return pltpu.CompilerParams(dimension_semantics=sem, vmem_limit_bytes=VMEM_LIMIT)


def _rms(x, g):
    return x * lax.rsqrt(jnp.mean(x * x, axis=-1, keepdims=True) + NORM_EPS) * g


def _silu(x):
    return x * jax.nn.sigmoid(x)


def _dot(a, b):
    return jnp.dot(a, b, preferred_element_type=F32)


def _dot_nt(a, b):
    return lax.dot_general(a, b, (((1,), (1,)), ((), ())), preferred_element_type=F32)


def _dot_tn(a, b):
    return lax.dot_general(a, b, (((0,), (0,)), ((), ())), preferred_element_type=F32)


def _modulated(x_ref, mod_ref, g_ref, k0, gi):
    m = mod_ref[0]
    y = _rms(x_ref[...], g_ref[gi:gi + 1, :])
    return (y * (1.0 + m[k0 + 1:k0 + 2, :]) + m[k0:k0 + 1, :]).astype(BF16)


def _ada_kernel(c_ref, w_ref, b_ref, o_ref):
    s = _silu(c_ref[...]).astype(BF16)
    o_ref[0] = _dot(s, w_ref[0].astype(BF16)) + b_ref[0]


def _ada(cs, ada_w, ada_b, tn=1024):
    depth, d, n = ada_w.shape
    r = cs.shape[0]
    return pl.pallas_call(
        _ada_kernel,
        grid=(depth, n // tn),
        in_specs=[
            pl.BlockSpec((r, d), lambda i, j: (0, 0)),
            pl.BlockSpec((1, d, tn), lambda i, j: (i, 0, j)),
            pl.BlockSpec((1, 1, tn), lambda i, j: (i, 0, j)),
        ],
        out_specs=pl.BlockSpec((1, r, tn), lambda i, j: (i, 0, j)),
        out_shape=jax.ShapeDtypeStruct((depth, r, n), F32),
        compiler_params=_cparams("parallel", "parallel"),
    )(cs, ada_w, ada_b.reshape(depth, 1, n))


def _with_tile_input(refs, k0, gi, chunk, body):
    x_ref, xn_ref, mod_ref, modn_ref, g_ref, h0_scr, h1_scr = refs
    i = pl.program_id(0)
    j = pl.program_id(1)
    tm = x_ref.shape[0]

    @pl.when(jnp.logical_and(i == 0, j == 0))
    def _():
        h0_scr[...] = _modulated(x_ref, mod_ref, g_ref, k0, gi)

    r0 = pl.multiple_of(jnp.minimum(j * chunk, tm - chunk), BF16_ROWS)

    def step(h_cur, h_next):
        h_next[pl.ds(r0, chunk), :] = _modulated(xn_ref.at[pl.ds(r0, chunk), :], modn_ref, g_ref, k0, gi)
        body(h_cur[...])

    @pl.when(i % 2 == 0)
    def _():
        step(h0_scr, h1_scr)

    @pl.when(i % 2 == 1)
    def _():
        step(h1_scr, h0_scr)


def _tile_input_specs(tm, d, rows_per_mod, g, n_tiles, n_steps, x_every_tile=True):
    chunk = -(-tm // (n_steps * BF16_ROWS)) * BF16_ROWS
    assert chunk <= tm and tm % BF16_ROWS == 0
    nxt = lambda i: jnp.minimum(i + 1, n_tiles - 1)
    x_spec = (pl.BlockSpec((tm, d), lambda i, j: (i, 0)) if x_every_tile else
              pl.BlockSpec((tm, d), lambda i, j: (0, 0), pipeline_mode=pl.Buffered(1)))
    specs = [
        x_spec,
        pl.BlockSpec((tm, d), lambda i, j: (nxt(i), 0)),
        pl.BlockSpec((1, N_MOD, d), lambda i, j: ((i * tm) // rows_per_mod, 0, 0)),
        pl.BlockSpec((1, N_MOD, d), lambda i, j: ((nxt(i) * tm) // rows_per_mod, 0, 0)),
        pl.BlockSpec(g.shape, lambda i, j: (0, 0)),
    ]
    return specs, [pltpu.VMEM((tm, d), BF16), pltpu.VMEM((tm, d), BF16)], chunk


def _ffn_kernel(x_ref, xn_ref, mod_ref, modn_ref, g_ref, wg_ref, wu_ref, wd_ref, o_ref,
                h0_scr, h1_scr, acc_scr, *, k0, gi, chunk):
    j = pl.program_id(1)

    @pl.when(j == 0)
    def _():
        acc_scr[...] = jnp.zeros_like(acc_scr)

    def body(h):
        gate = _dot(h, wg_ref[...])
        up = _dot(h, wu_ref[...])
        a = (_silu(gate) * up).astype(BF16)
        acc_scr[...] += _dot(a, wd_ref[...])

    _with_tile_input((x_ref, xn_ref, mod_ref, modn_ref, g_ref, h0_scr, h1_scr), k0, gi, chunk, body)

    @pl.when(j == pl.num_programs(1) - 1)
    def _():
        m = mod_ref[0]
        y = _rms(acc_scr[...], g_ref[gi + 1:gi + 2, :])
        o_ref[...] = x_ref[...] + (0.5 * m[k0 + 2:k0 + 3, :]) * y


def _ffn(x, mod, rows_per_mod, g, wg, wu, wd, *, layer, which, tm=512, tf=512):
    m_rows, d = x.shape
    f = wg.shape[-1]
    tm = min(tm, rows_per_mod)
    k0, gi = (0, 0) if which == 0 else (6, 4)
    n_tiles, n_steps = m_rows // tm, f // tf
    tile_specs, h_scratch, chunk = _tile_input_specs(tm, d, rows_per_mod, g, n_tiles, n_steps)
    return pl.pallas_call(
        functools.partial(_ffn_kernel, k0=k0, gi=gi, chunk=chunk),
        grid=(n_tiles, n_steps),
        in_specs=tile_specs + [
            pl.BlockSpec((None, None, d, tf), lambda i, j: (layer, which, 0, j)),
            pl.BlockSpec((None, None, d, tf), lambda i, j: (layer, which, 0, j)),
            pl.BlockSpec((None, None, tf, d), lambda i, j: (layer, which, j, 0)),
        ],
        out_specs=pl.BlockSpec((tm, d), lambda i, j: (i, 0)),
        out_shape=jax.ShapeDtypeStruct((m_rows, d), F32),
        scratch_shapes=h_scratch + [pltpu.VMEM((tm, d), F32)],
        compiler_params=_cparams("arbitrary", "arbitrary"),
    )(x, x, mod, mod, g, wg, wu, wd)


def _ffn_stream_kernel(x_ref, xn_ref, mod_ref, modn_ref, g_ref, wg_ref, wu_ref, wd_hbm, o_ref,
                       h0_scr, h1_scr, acc_scr, wd_buf, wd_sem, *, k0, gi, chunk, layer, which):
    i = pl.program_id(0)
    j = pl.program_id(1)
    n_steps = pl.num_programs(1)
    tf = wd_buf.shape[1]
    step = i * n_steps + j
    slot = step % 2

    def wd_copy(block, into):
        return pltpu.make_async_copy(
            wd_hbm.at[layer, which, pl.ds(pl.multiple_of(block * tf, tf), tf), :], wd_buf.at[into], wd_sem.at[into])

    @pl.when(step == 0)
    def _():
        wd_copy(0, 0).start(priority=1)

    @pl.when(step + 1 < pl.num_programs(0) * n_steps)
    def _():
        wd_copy((j + 1) % n_steps, 1 - slot).start(priority=1)

    @pl.when(j == 0)
    def _():
        acc_scr[...] = jnp.zeros_like(acc_scr)

    wd_copy(j, slot).wait()

    def body(h):
        gate = _dot(h, wg_ref[...])
        up = _dot(h, wu_ref[...])
        a = (_silu(gate) * up).astype(BF16)
        acc_scr[...] += _dot(a, wd_buf[slot])

    _with_tile_input((x_ref, xn_ref, mod_ref, modn_ref, g_ref, h0_scr, h1_scr), k0, gi, chunk, body)

    @pl.when(j == n_steps - 1)
    def _():
        m = mod_ref[0]
        y = _rms(acc_scr[...], g_ref[gi + 1:gi + 2, :])
        o_ref[...] = x_ref[...] + (0.5 * m[k0 + 2:k0 + 3, :]) * y


def _ffn_stream(x, mod, rows_per_mod, g, wg, wu, wd, *, layer, which, tm=512, tf=512):
    m_rows, d = x.shape
    f = wg.shape[-1]
    tm = min(tm, rows_per_mod)
    k0, gi = (0, 0) if which == 0 else (6, 4)
    n_tiles, n_steps = m_rows // tm, f // tf
    tile_specs, h_scratch, chunk = _tile_input_specs(tm, d, rows_per_mod, g, n_tiles, n_steps)
    return pl.pallas_call(
        functools.partial(_ffn_stream_kernel, k0=k0, gi=gi, chunk=chunk, layer=layer, which=which),
        grid=(n_tiles, n_steps),
        in_specs=tile_specs + [
            pl.BlockSpec((None, None, d, tf), lambda i, j: (layer, which, 0, j)),
            pl.BlockSpec((None, None, d, tf), lambda i, j: (layer, which, 0, j)),
            pl.BlockSpec(memory_space=pl.ANY),
        ],
        out_specs=pl.BlockSpec((tm, d), lambda i, j: (i, 0)),
        out_shape=jax.ShapeDtypeStruct((m_rows, d), F32),
        scratch_shapes=h_scratch + [pltpu.VMEM((tm, d), F32), pltpu.VMEM((2, tf, d), BF16),
                                    pltpu.SemaphoreType.DMA((2,))],
        compiler_params=_cparams("arbitrary", "arbitrary"),
    )(x, x, mod, mod, g, wg, wu, wd)


def _proj_plain_kernel(x_ref, xn_ref, mod_ref, modn_ref, g_ref, w_ref, o_ref, h0_scr, h1_scr,
                       *, k0, gi, chunk):
    def body(h):
        o_ref[...] = _dot(h, w_ref[...]).astype(o_ref.dtype)

    _with_tile_input((x_ref, xn_ref, mod_ref, modn_ref, g_ref, h0_scr, h1_scr), k0, gi, chunk, body)


def _proj_glu_kernel(x_ref, xn_ref, mod_ref, modn_ref, g_ref, wa_ref, wg_ref, ba_ref, bg_ref, o_ref,
                     h0_scr, h1_scr, *, k0, gi, chunk):
    def body(h):
        for c0 in range(0, o_ref.shape[1], GLU_COL_CHUNK):
            cols = slice(c0, c0 + GLU_COL_CHUNK)
            a = _dot(h, wa_ref[:, cols]) + ba_ref[:, cols]
            gate = _dot(h, wg_ref[:, cols]) + bg_ref[:, cols]
            o_ref[:, cols] = a * jax.nn.sigmoid(gate)

    _with_tile_input((x_ref, xn_ref, mod_ref, modn_ref, g_ref, h0_scr, h1_scr), k0, gi, chunk, body)


def _rope_groups(acc, cos_ref, sin_ref, scale, o_ref):
    n_groups = acc.shape[1] // LANE
    for gidx in range(n_groups):
        xg = acc[:, gidx * LANE:(gidx + 1) * LANE]
        t = (gidx % 2) * LANE
        rot = pltpu.roll(xg, LANE // 2, axis=1)
        o_ref[:, gidx * LANE:(gidx + 1) * LANE] = (
            xg * cos_ref[:, t:t + LANE] + rot * sin_ref[:, t:t + LANE]) * scale


def _proj_gla_kernel(x_ref, xn_ref, mod_ref, modn_ref, g_ref, w_ref, b_ref, cos_ref, sin_ref, o_ref,
                     h0_scr, h1_scr, *, k0, gi, chunk, q_scale, n_qk, n_v, rope):
    j = pl.program_id(1)

    def body(h):
        acc = _dot(h, w_ref[...])

        @pl.when(j < 2 * n_qk)
        def _():
            scale = jnp.where(j < n_qk, q_scale, 1.0).astype(F32)
            if rope:
                _rope_groups(acc, cos_ref, sin_ref, scale, o_ref)
            else:
                o_ref[...] = acc * scale

        @pl.when(jnp.logical_and(j >= 2 * n_qk, j < 2 * n_qk + n_v))
        def _():
            o_ref[...] = acc

        @pl.when(j >= 2 * n_qk + n_v)
        def _():
            o_ref[...] = _silu(acc + b_ref[...])

    _with_tile_input((x_ref, xn_ref, mod_ref, modn_ref, g_ref, h0_scr, h1_scr), k0, gi, chunk, body)


def _gla_gate_kernel(x_ref, mod_ref, g_ref, a1_ref, a2_ref, ba_ref, o_ref, *, k0, gi):
    h = _modulated(x_ref, mod_ref, g_ref, k0, gi)
    low = _dot(h, a1_ref[...]).astype(BF16)
    for d in range(2):
        z = _dot(low, a2_ref[d]) + ba_ref[d]
        log_sig = jnp.minimum(z, 0.0) - jnp.log1p(jnp.exp(-jnp.abs(z)))
        o_ref[d] = log_sig * (1.0 / GLA_GATE_NORMALIZER)


def _proj_call(kernel_fn, extra_specs, extra_args, x, mod, rows_per_mod, g, n, tn, out_dtype, tm, **static):
    m_rows, d = x.shape
    tm = min(tm, rows_per_mod)
    n_tiles, n_steps = m_rows // tm, n // tn
    tile_specs, h_scratch, chunk = _tile_input_specs(tm, d, rows_per_mod, g, n_tiles, n_steps,
                                                     x_every_tile=False)
    return pl.pallas_call(
        functools.partial(kernel_fn, chunk=chunk, **static),
        grid=(n_tiles, n_steps),
        in_specs=tile_specs + extra_specs(tm),
        out_specs=pl.BlockSpec((tm, tn), lambda i, j: (i, j)),
        out_shape=jax.ShapeDtypeStruct((m_rows, n), out_dtype),
        scratch_shapes=h_scratch,
        compiler_params=_cparams("arbitrary", "arbitrary"),
    )(x, x, mod, mod, g, *extra_args)


def _proj_plain(x, mod, rows_per_mod, g, w, *, k0, gi, out_dtype, tm=1024, tn=512):
    d, n = w.shape
    specs = lambda tm: [pl.BlockSpec((d, tn), lambda i, j: (0, j))]
    return _proj_call(_proj_plain_kernel, specs, (w,), x, mod, rows_per_mod, g, n, tn, out_dtype, tm,
                      k0=k0, gi=gi)


def _proj_glu(x, mod, rows_per_mod, g, w, b, *, k0, gi, tm=512, tn=512):
    d = w.shape[0]
    n = w.shape[1] // 2
    nb = n // tn
    assert tn % GLU_COL_CHUNK == 0
    b2 = b.reshape(1, 2 * n)
    mode = dict(pipeline_mode=pl.Buffered(1)) if nb == 1 else {}
    specs = lambda tm: [
        pl.BlockSpec((d, tn), lambda i, j: (0, j), **mode),
        pl.BlockSpec((d, tn), lambda i, j: (0, j + nb), **mode),
        pl.BlockSpec((1, tn), lambda i, j: (0, j), **mode),
        pl.BlockSpec((1, tn), lambda i, j: (0, j + nb), **mode),
    ]
    return _proj_call(_proj_glu_kernel, specs, (w, w, b2, b2), x, mod, rows_per_mod, g, n, tn, F32, tm,
                      k0=k0, gi=gi)


def _proj_gla(x, mod, rows_per_mod, g, w_cat, b_r, cos, sin, *, k0, gi, dk_total, dv_total, rope,
              tm=1024, tn=512):
    d, n = w_cat.shape
    n_qk, n_v = dk_total // tn, dv_total // tn
    dk = dk_total // GLA_HEADS

    def specs(tm):
        seq_blocks = cos.shape[0] // tm
        tab_spec = pl.BlockSpec((tm, cos.shape[1]), lambda i, j: (i % seq_blocks, 0))
        return [
            pl.BlockSpec((d, tn), lambda i, j: (0, j)),
            pl.BlockSpec((1, tn), lambda i, j: (0, jnp.maximum(j - (2 * n_qk + n_v), 0))),
            tab_spec, tab_spec,
        ]

    return _proj_call(_proj_gla_kernel, specs, (w_cat, b_r.reshape(1, -1), cos, sin), x, mod, rows_per_mod, g,
                      n, tn, F32, tm, k0=k0, gi=gi, q_scale=dk ** -0.5, n_qk=n_qk, n_v=n_v, rope=rope)


def _gla_gate(x, mod, rows_per_mod, g, a1_cat, a2_pad, b_a, *, k0, gi, tm=256):
    m_rows, d = x.shape
    n = a2_pad.shape[2]
    tm = min(tm, rows_per_mod)
    return pl.pallas_call(
        functools.partial(_gla_gate_kernel, k0=k0, gi=gi),
        grid=(m_rows // tm,),
        in_specs=[
            pl.BlockSpec((tm, d), lambda i: (i, 0)),
            pl.BlockSpec((1, N_MOD, d), lambda i: ((i * tm) // rows_per_mod, 0, 0)),
            pl.BlockSpec(g.shape, lambda i: (0, 0)),
            pl.BlockSpec(a1_cat.shape, lambda i: (0, 0)),
            pl.BlockSpec(a2_pad.shape, lambda i: (0, 0, 0)),
            pl.BlockSpec((2, 1, n), lambda i: (0, 0, 0)),
        ],
        out_specs=pl.BlockSpec((2, tm, n), lambda i: (0, i, 0)),
        out_shape=jax.ShapeDtypeStruct((2, m_rows, n), F32),
        compiler_params=_cparams("parallel"),
    )(x, mod, g, a1_cat, a2_pad, b_a.reshape(2, 1, n))


def _gla_scan_kernel(q_ref, k_ref, v_ref, la_ref, s0_ref, o_ref, sf_ref, s_scr, *, n_sub, dk, dv):
    d = pl.program_id(1)
    c = pl.program_id(2)
    fwd = d == 0

    @pl.when(c == 0)
    def _():
        s_scr[...] = s0_ref[0, 0]

    ch = GLA_CHUNK
    row = lax.broadcasted_iota(jnp.int32, (ch, ch), 0)
    col = lax.broadcasted_iota(jnp.int32, (ch, ch), 1)
    tri = (row - col) * (1 - 2 * d) >= 0
    tri_f = tri.astype(F32)

    def sub(i, carry):
        r0 = pl.multiple_of(jnp.where(fwd, i, n_sub - 1 - i) * ch, ch)
        for h in range(GLA_HEADS):
            qc = q_ref[pl.ds(r0, ch), h * dk:(h + 1) * dk]
            kc = k_ref[pl.ds(r0, ch), h * dk:(h + 1) * dk]
            vc = v_ref[pl.ds(r0, ch), h * dv:(h + 1) * dv].astype(BF16)
            ac = la_ref[0, pl.ds(r0, ch), h * dk:(h + 1) * dk]
            cum = jnp.dot(tri_f, ac, precision=lax.Precision.HIGHEST, preferred_element_type=F32)
            total = jnp.sum(ac, axis=0, keepdims=True)
            qe = (qc * jnp.exp(cum)).astype(BF16)
            ke = (kc * jnp.exp(-cum)).astype(BF16)
            kd = (kc * jnp.exp(total - cum)).astype(BF16)
            scores = jnp.where(tri, _dot_nt(qe, ke), 0.0).astype(BF16)
            s_old = s_scr[h]
            o_ref[0, pl.ds(r0, ch), h * dv:(h + 1) * dv] = _dot(qe, s_old.astype(BF16)) + _dot(scores, vc)
            decay_t = jnp.transpose(jnp.broadcast_to(jnp.exp(total), (LANE, dk)))
            decay = jnp.concatenate([decay_t] * (dv // LANE), axis=1)
            s_scr[h] = decay * s_old + _dot_tn(kd, vc)
        return carry

    lax.fori_loop(0, n_sub, sub, 0)

    @pl.when(c == pl.num_programs(2) - 1)
    def _():
        sf_ref[0, 0] = s_scr[...]


def _gla_scan(p, la, s0, *, batch, seq, dk, dv, n_sub=4):
    h = GLA_HEADS
    n_sub = min(n_sub, seq // GLA_CHUNK)
    rb = n_sub * GLA_CHUNK
    nblk = seq // rb
    kq, kv = h * dk, h * dv
    blk = lambda b, d, c: b * nblk + c + d * (nblk - 1 - 2 * c)
    o, s_fin = pl.pallas_call(
        functools.partial(_gla_scan_kernel, n_sub=n_sub, dk=dk, dv=dv),
        grid=(batch, 2, nblk),
        in_specs=[
            pl.BlockSpec((rb, kq), lambda b, d, c: (blk(b, d, c), 0)),
            pl.BlockSpec((rb, kq), lambda b, d, c: (blk(b, d, c), 1)),
            pl.BlockSpec((rb, kv), lambda b, d, c: (blk(b, d, c), (2 * kq) // kv)),
            pl.BlockSpec((1, rb, kq), lambda b, d, c: (d, blk(b, d, c), 0)),
            pl.BlockSpec((1, 1, h, dk, dv), lambda b, d, c: (b, d, 0, 0, 0)),
        ],
        out_specs=[
            pl.BlockSpec((1, rb, kv), lambda b, d, c: (d, blk(b, d, c), 0)),
            pl.BlockSpec((1, 1, h, dk, dv), lambda b, d, c: (b, d, 0, 0, 0)),
        ],
        out_shape=[
            jax.ShapeDtypeStruct((2, batch * seq, kv), F32),
            jax.ShapeDtypeStruct((batch, 2, h, dk, dv), F32),
        ],
        scratch_shapes=[pltpu.VMEM((h, dk, dv), F32)],
        compiler_params=_cparams("parallel", "parallel", "arbitrary"),
    )(p, p, p, la, s0)
    return o, s_fin


def _na_kernel(q_ref, k_ref, v_ref, kc_ref, vc_ref, bias_ref, o_ref, *, rows, scale, plan):
    kc = kc_ref[...]
    vc = vc_ref[...]
    w = GRID_W
    n_q = NA_BLOCK_ROWS * w
    n_k = (NA_BLOCK_ROWS + NA_WIN_R) * w
    common, exceptions = plan

    def block(b, carry):
        r0 = b * NA_BLOCK_ROWS
        ws = jnp.clip(r0 - NA_WIN_R // 2, 0, rows - (NA_BLOCK_ROWS + NA_WIN_R))
        pid = common
        for blk, p in exceptions:
            pid = jnp.where(b == blk, p, pid)
        q0 = pl.multiple_of(r0 * w, w)
        k0 = pl.multiple_of(ws * w, w)
        q = q_ref[pl.ds(q0, n_q), :]
        kw = k_ref[pl.ds(k0, n_k), :]
        vw = v_ref[pl.ds(k0, n_k), :]
        s_lat = _dot_nt(q, kw) * scale + bias_ref[0, pid]
        s_ctx = _dot_nt(q, kc) * scale
        m = jnp.maximum(jnp.max(s_lat, axis=-1, keepdims=True), jnp.max(s_ctx, axis=-1, keepdims=True))
        p_lat = jnp.exp(s_lat - m)
        p_ctx = jnp.exp(s_ctx - m)
        denom = jnp.sum(p_lat, axis=-1, keepdims=True) + jnp.sum(p_ctx, axis=-1, keepdims=True)
        acc = _dot(p_lat.astype(BF16), vw) + _dot(p_ctx.astype(BF16), vc)
        o_ref[pl.ds(q0, n_q), :] = (acc / denom).astype(o_ref.dtype)
        return carry

    lax.fori_loop(0, rows // NA_BLOCK_ROWS, block, 0, unroll=NA_BLOCK_UNROLL)


def _na_plan(rows):
    kw_rows = NA_BLOCK_ROWS + NA_WIN_R
    patterns, ids = [], []
    for b in range(rows // NA_BLOCK_ROWS):
        r0 = b * NA_BLOCK_ROWS
        ws = min(max(r0 - NA_WIN_R // 2, 0), rows - kw_rows)
        key = []
        for rho in range(NA_BLOCK_ROWS):
            r = r0 + rho
            rs = min(max(r - NA_WIN_R // 2, 0), rows - NA_WIN_R)
            assert ws <= rs and rs + NA_WIN_R <= ws + kw_rows
            key.append((rs - ws, ws - r + NA_WIN_R - 1))
        key = tuple(key)
        if key not in patterns:
            patterns.append(key)
        ids.append(patterns.index(key))
    common = max(set(ids), key=ids.count)
    exceptions = tuple((b, p) for b, p in enumerate(ids) if p != common)
    dr = [[[min(max(base + om, 0), 2 * NA_WIN_R - 2) for om in range(kw_rows)] for (_, base) in pat]
          for pat in patterns]
    valid = [[[first <= om < first + NA_WIN_R for om in range(kw_rows)] for (first, _) in pat]
             for pat in patterns]
    return dr, valid, (common, exceptions)


def _na_bias_table(rpb, rows):
    w = GRID_W
    dr, valid_rows, plan = _na_plan(rows)
    col = jnp.arange(w)
    cs = jnp.clip(col - NA_WIN_C // 2, 0, w - NA_WIN_C)
    valid_cols = (col[None, :] >= cs[:, None]) & (col[None, :] < cs[:, None] + NA_WIN_C)
    rel = jnp.clip(col[None, :] - col[:, None] + (NA_WIN_C - 1), 0, 2 * NA_WIN_C - 2)
    cb = jnp.where(valid_cols[None, None], rpb[:, :, rel], MASK_BIAS)
    t = cb[:, jnp.asarray(dr)]
    t = jnp.where(jnp.asarray(valid_rows)[None, :, :, :, None, None], t, MASK_BIAS)
    n_pat = len(dr)
    t = t.transpose(0, 1, 2, 4, 3, 5).reshape(
        rpb.shape[0], n_pat, NA_BLOCK_ROWS * w, (NA_BLOCK_ROWS + NA_WIN_R) * w)
    return t.astype(F32), plan


def _na_attention(qkv, qkv_ctx, bias, plan, *, batch, seq, ctx_len, d):
    hd = d // NA_HEADS
    rows = seq // GRID_W
    assert rows >= NA_BLOCK_ROWS + NA_WIN_R and rows % NA_BLOCK_ROWS == 0 and hd == LANE
    nh = NA_HEADS
    return pl.pallas_call(
        functools.partial(_na_kernel, rows=rows, scale=hd ** -0.5, plan=plan),
        grid=(batch, nh),
        in_specs=[
            pl.BlockSpec((seq, hd), lambda b, h: (b, h)),
            pl.BlockSpec((seq, hd), lambda b, h: (b, nh + h)),
            pl.BlockSpec((seq, hd), lambda b, h: (b, 2 * nh + h)),
            pl.BlockSpec((ctx_len, hd), lambda b, h: (b, nh + h)),
            pl.BlockSpec((ctx_len, hd), lambda b, h: (b, 2 * nh + h)),
            pl.BlockSpec((1,) + bias.shape[1:], lambda b, h: (h, 0, 0, 0)),
        ],
        out_specs=pl.BlockSpec((seq, hd), lambda b, h: (b, h)),
        out_shape=jax.ShapeDtypeStruct((batch * seq, d), BF16),
        compiler_params=_cparams("parallel", "parallel"),
    )(qkv, qkv, qkv, qkv_ctx, qkv_ctx, bias)


def _finish(a, w_ref, b_ref, x_ref, mod_ref, g_ref, o_ref):
    y = _dot(a, w_ref[...])
    if b_ref is not None:
        y = y + b_ref[...]
    m = mod_ref[0]
    o_ref[...] = x_ref[...] + m[5:6, :] * _rms(y, g_ref[3:4, :])


def _out_na_kernel(a_ref, w_ref, x_ref, mod_ref, g_ref, o_ref):
    _finish(a_ref[...], w_ref, None, x_ref, mod_ref, g_ref, o_ref)


def _out_gla_kernel(of_ref, ob_ref, r_ref, ng_ref, w_ref, x_ref, mod_ref, g_ref, o_ref, a_scr, *, dv):
    ng = ng_ref[...]
    for h in range(GLA_HEADS):
        sl = slice(h * dv, (h + 1) * dv)
        o = of_ref[0, :, sl] + ob_ref[0, :, sl]
        a_scr[:, sl] = (_rms(o, ng) * r_ref[:, sl]).astype(BF16)
    _finish(a_scr[...], w_ref, None, x_ref, mod_ref, g_ref, o_ref)


def _out_conv_kernel(prev_ref, cur_ref, next_ref, wdw_ref, bdw_ref, lng_ref, lnb_ref, w_ref, b_ref,
                     x_ref, mod_ref, g_ref, o_ref, ext_scr, u_scr, *, seq, width, rb, ls):
    tm = cur_ref.shape[0]
    pos = (pl.program_id(0) * tm) % seq
    halo = CONV_HALO
    ext_scr[0:halo, :] = jnp.where(pos > 0, prev_ref[...], 0.0)
    ext_scr[halo:halo + tm, :] = cur_ref[...]
    ext_scr[halo + tm:, :] = jnp.where(pos + tm < seq, next_ref[...], 0.0)
    off = halo - width // 2
    d = cur_ref.shape[1]
    n_q = (off + width - 1) // SUBLANE + 1

    def rows(i, carry):
        r0 = pl.multiple_of(i * rb, rb)
        for c0 in range(0, d, ls):
            v = ext_scr[pl.ds(r0, rb + 2 * halo), c0:c0 + ls]
            acc = jnp.broadcast_to(bdw_ref[:, c0:c0 + ls], (rb, ls))
            for s in range(SUBLANE):
                vs = v[s:s + rb + SUBLANE * (n_q - 1)]
                for q in range(n_q):
                    k = SUBLANE * q + s - off
                    if 0 <= k < width:
                        wk = jnp.concatenate([wdw_ref[k, :, c0:c0 + ls]] * (rb // SUBLANE), axis=0)
                        acc = acc + wk * vs[SUBLANE * q:SUBLANE * q + rb]
            u_scr[pl.ds(r0, rb), c0:c0 + ls] = acc
        return carry

    lax.fori_loop(0, tm // rb, rows, 0)
    u = u_scr[...]
    mu = jnp.mean(u, axis=-1, keepdims=True)
    uc = u - mu
    var = jnp.mean(uc * uc, axis=-1, keepdims=True)
    y = uc * lax.rsqrt(var + NORM_EPS) * lng_ref[...] + lnb_ref[...]
    _finish(_silu(y).astype(BF16), w_ref, b_ref, x_ref, mod_ref, g_ref, o_ref)


def _out_tail_specs(tm, d, rows_per_mod, g):
    return [
        pl.BlockSpec((d, d), lambda i: (0, 0)),
        pl.BlockSpec((tm, d), lambda i: (i, 0)),
        pl.BlockSpec((1, N_MOD, d), lambda i: ((i * tm) // rows_per_mod, 0, 0)),
        pl.BlockSpec(g.shape, lambda i: (0, 0)),
    ]


def _out_call(kernel, head_specs, head_args, w, x, mod, rows_per_mod, g, tm, scratch=()):
    m_rows, d = x.shape
    tail = _out_tail_specs(tm, d, rows_per_mod, g)
    return pl.pallas_call(
        kernel,
        grid=(m_rows // tm,),
        in_specs=head_specs + tail,
        out_specs=pl.BlockSpec((tm, d), lambda i: (i, 0)),
        out_shape=jax.ShapeDtypeStruct((m_rows, d), F32),
        scratch_shapes=list(scratch),
        compiler_params=_cparams("parallel"),
    )(*head_args, w, x, mod, g)


def _out_na(a, w, x, mod, rows_per_mod, g, tm=512):
    d = x.shape[1]
    tm = min(tm, rows_per_mod)
    return _out_call(_out_na_kernel, [pl.BlockSpec((tm, d), lambda i: (i, 0))], [a],
                     w, x, mod, rows_per_mod, g, tm)


def _out_gla(o, p, norm_g, w, x, mod, rows_per_mod, g, *, dv, tm=256):
    d = x.shape[1]
    tm = min(tm, rows_per_mod)
    r_blk = (p.shape[1] - d) // d
    specs = [
        pl.BlockSpec((1, tm, d), lambda i: (0, i, 0)),
        pl.BlockSpec((1, tm, d), lambda i: (1, i, 0)),
        pl.BlockSpec((tm, d), lambda i: (i, r_blk)),
        pl.BlockSpec((1, dv), lambda i: (0, 0)),
    ]
    return _out_call(functools.partial(_out_gla_kernel, dv=dv), specs, [o, o, p, norm_g.reshape(1, dv)],
                     w, x, mod, rows_per_mod, g, tm, scratch=[pltpu.VMEM((tm, d), BF16)])


def _out_conv(u, w_dw, b_dw, ln_g, ln_b, w, b, x, mod, rows_per_mod, g, tm=256,
              rb=CONV_ROW_BLOCK, ls=CONV_LANE_BLOCK):
    m_rows, d = x.shape
    tm = min(tm, rows_per_mod)
    width = w_dw.shape[0]
    assert width // 2 < CONV_HALO and tm % CONV_HALO == 0 and tm % rb == 0 and d % ls == 0
    hb = tm // CONV_HALO
    last = m_rows // CONV_HALO - 1
    row = lambda v: v.reshape(1, d)
    specs = [
        pl.BlockSpec((CONV_HALO, d), lambda i: (jnp.maximum(i * hb - 1, 0), 0)),
        pl.BlockSpec((tm, d), lambda i: (i, 0)),
        pl.BlockSpec((CONV_HALO, d), lambda i: (jnp.minimum((i + 1) * hb, last), 0)),
        pl.BlockSpec((width, SUBLANE, d), lambda i: (0, 0, 0)),
    ] + [pl.BlockSpec((1, d), lambda i: (0, 0))] * 3
    kernel = functools.partial(_out_conv_kernel, seq=rows_per_mod, width=width, rb=rb, ls=ls)
    tail = _out_tail_specs(tm, d, rows_per_mod, g)
    return pl.pallas_call(
        kernel,
        grid=(m_rows // tm,),
        in_specs=specs + [tail[0], pl.BlockSpec((1, d), lambda i: (0, 0))] + tail[1:],
        out_specs=pl.BlockSpec((tm, d), lambda i: (i, 0)),
        out_shape=jax.ShapeDtypeStruct((m_rows, d), F32),
        scratch_shapes=[pltpu.VMEM((tm + 2 * CONV_HALO, d), F32), pltpu.VMEM((tm, d), F32)],
        compiler_params=_cparams("parallel"),
    )(u, u, u, jnp.broadcast_to(w_dw[:, None, :], (width, SUBLANE, d)),
      row(b_dw), row(ln_g), row(ln_b), w, row(b), x, mod, g)


def _rope_tables(seq, dk):
    n = dk // 2
    inv = ROPE_THETA ** (-jnp.arange(0, n, 2, dtype=F32) / n)
    t = jnp.arange(seq)

    def half(pos):
        ang = pos.astype(F32)[:, None] * inv
        c, s = jnp.cos(ang), jnp.sin(ang)
        return jnp.concatenate([c, c], axis=-1), jnp.concatenate([-s, s], axis=-1)

    cr, sr = half(t // GRID_W)
    cc, sc = half(t % GRID_W)
    return jnp.concatenate([cr, cc], axis=-1), jnp.concatenate([sr, sc], axis=-1)


def kernel(x, c, ctx, c_ctx, ada_w, ada_b, norm_g, ffn_w_gate, ffn_w_up, ffn_w_down, conv_w_in, conv_b_in, conv_w_dw, conv_b_dw, conv_ln_g, conv_ln_b, conv_w_out, conv_b_out, gla_w_q, gla_w_k, gla_w_v, gla_w_r, gla_b_r, gla_w_a1, gla_w_a2, gla_b_a, gla_norm_g, gla_w_o, na_w_qkv, na_rpb, na_w_o):
    batch, seq, d = x.shape
    ctx_len = ctx.shape[1]
    depth = ada_w.shape[0]
    dk_total = gla_w_q.shape[-1]
    dv_total = gla_w_v.shape[-1]
    dk, dv = dk_total // GLA_HEADS, dv_total // GLA_HEADS

    pad = (-(batch + 1)) % 8
    cs = jnp.concatenate([c, c_ctx[None, :], jnp.zeros((pad, d), F32)], axis=0)
    mods = _ada(cs, ada_w, ada_b).reshape(depth, batch + 1 + pad, N_MOD, d)

    wg_all, wu_all, wd_all = (w.astype(BF16) for w in (ffn_w_gate, ffn_w_up, ffn_w_down))

    x_lat = x.reshape(batch * seq, d)
    x_ctx = ctx.reshape(batch * ctx_len, d)
    reads_ctx = [(i % N_MIXERS) != 0 for i in range(depth)]

    for i in range(depth):
        kind, j = i % N_MIXERS, i // N_MIXERS
        ctx_out = any(reads_ctx[i + 1:])
        ctx_in = ctx_out or reads_ctx[i]
        mod_lat = mods[i, :batch]
        mod_ctx = jnp.broadcast_to(mods[i, batch], (batch, N_MOD, d))
        g = norm_g[i]
        lat = dict(mod=mod_lat, rows_per_mod=seq, g=g)
        cx = dict(mod=mod_ctx, rows_per_mod=ctx_len, g=g)

        ffn1 = dict(wg=wg_all, wu=wu_all, wd=wd_all, layer=i, which=0)
        ffn2 = dict(wg=wg_all, wu=wu_all, wd=wd_all, layer=i, which=1)

        cx_ffn = dict(mod=mod_ctx[:1], rows_per_mod=batch * ctx_len, g=g)
        if i == 2:
            x_lat = _ffn_stream(x_lat, **lat, **ffn1)
        else:
            x_lat = _ffn(x_lat, **lat, **ffn1)
        if ctx_in:
            x_ctx = _ffn(x_ctx, **cx_ffn, **ffn1)

        x_ctx_mixed = None
        if kind == 0:
            w_in = conv_w_in[j].astype(BF16)
            w_out = conv_w_out[j].astype(BF16)
            tail = (conv_w_dw[j], conv_b_dw[j], conv_ln_g[j], conv_ln_b[j], w_out, conv_b_out[j])
            u = _proj_glu(x_lat, w=w_in, b=conv_b_in[j], k0=3, gi=2, **lat, tn=512 if j == 0 else 2048)
            x_lat = _out_conv(u, *tail, x_lat, mod_lat, seq, g)
            if ctx_out:
                uc = _proj_glu(x_ctx, w=w_in, b=conv_b_in[j], k0=3, gi=2, **cx)
                x_ctx_mixed = _out_conv(uc, *tail, x_ctx, mod_ctx, ctx_len, g)
        elif kind == 1:
            w_cat = jnp.concatenate([gla_w_q[j], gla_w_k[j], gla_w_v[j], gla_w_r[j]], axis=1).astype(BF16)
            a1_cat = jnp.concatenate(
                [gla_w_a1[j, 0], gla_w_a1[j, 1], jnp.zeros((d, LANE - 2 * GLA_GATE_RANK), F32)], axis=1).astype(BF16)
            a2_pad = jnp.zeros((2, LANE, dk_total), F32)
            for dd in range(2):
                a2_pad = a2_pad.at[dd, dd * GLA_GATE_RANK:(dd + 1) * GLA_GATE_RANK].set(gla_w_a2[j, dd])
            a2_pad = a2_pad.astype(BF16)
            w_o = gla_w_o[j].astype(BF16)
            cos, sin = _rope_tables(seq, dk)
            proj = dict(w_cat=w_cat, b_r=gla_b_r[j], k0=3, gi=2, dk_total=dk_total, dv_total=dv_total)
            gate = dict(a1_cat=a1_cat, a2_pad=a2_pad, b_a=gla_b_a[j], k0=3, gi=2)

            p_ctx = _proj_gla(x_ctx, cos=cos, sin=sin, rope=False, **cx, **proj)
            la_ctx = _gla_gate(x_ctx, **cx, **gate)
            s0 = jnp.zeros((batch, 2, GLA_HEADS, dk, dv), F32)
            o_ctx, s_ctx = _gla_scan(p_ctx, la_ctx, s0, batch=batch, seq=ctx_len, dk=dk, dv=dv)

            p_lat = _proj_gla(x_lat, cos=cos, sin=sin, rope=True, **lat, **proj)
            la_lat = _gla_gate(x_lat, **lat, **gate)
            o_lat, _ = _gla_scan(p_lat, la_lat, s_ctx, batch=batch, seq=seq, dk=dk, dv=dv)

            x_lat = _out_gla(o_lat, p_lat, gla_norm_g[j], w_o, x_lat, mod_lat, seq, g, dv=dv)
            if ctx_out:
                x_ctx_mixed = _out_gla(o_ctx, p_ctx, gla_norm_g[j], w_o, x_ctx, mod_ctx, ctx_len, g, dv=dv)
        else:
            assert not ctx_out, "context self-attention output is not needed for this depth"
            w_qkv = na_w_qkv[j].astype(BF16)
            qkv = _proj_plain(x_lat, w=w_qkv, k0=3, gi=2, out_dtype=BF16, **lat)
            qkv_ctx = _proj_plain(x_ctx, w=w_qkv, k0=3, gi=2, out_dtype=BF16, **cx)
            bias, plan = _na_bias_table(na_rpb[j], seq // GRID_W)
            a = _na_attention(qkv, qkv_ctx, bias, plan, batch=batch, seq=seq, ctx_len=ctx_len, d=d)
            x_lat = _out_na(a, na_w_o[j].astype(BF16), x_lat, mod_lat, seq, g)

        if i in (1, 2):
            x_lat = _ffn_stream(x_lat, **lat, **ffn2)
        else:
            x_lat = _ffn(x_lat, **lat, **ffn2, tf=256 if i == 0 else 512)
        if ctx_out:
            x_ctx = _ffn(x_ctx_mixed, **cx_ffn, **ffn2)

    return x_lat.reshape(batch, seq, d)
```

```python
import functools
import math

import jax
import jax.numpy as jnp
import numpy as np
from jax import lax
from jax.experimental import pallas as pl
from jax.experimental.pallas import tpu as pltpu

F32 = jnp.float32
BF16 = jnp.bfloat16

N_MIXERS = 3
N_MOD = 9
NORM_EPS = 1e-6
GRID_W = 64
GLA_HEADS = 4
GLA_GATE_RANK = 16
GLA_GATE_NORMALIZER = 16.0
GLA_CHUNK = 64
ROPE_THETA = 10000.0
NA_HEADS = 16
NA_WIN_R = 8
NA_WIN_C = 16
MASK_BIAS = -1e30
NA_BLOCK_ROWS = 4
NA_BLOCK_UNROLL = 4

LANE = 128
SUBLANE = 8
BF16_ROWS = 16
CONV_HALO = 16
CONV_ROW_BLOCK = 64
CONV_LANE_BLOCK = 128
GLU_COL_CHUNK = 512
VMEM_LIMIT = 56 * 1024 * 1024


def _cparams(*sem):
    return pltpu.CompilerParams(dimension_semantics=sem, vmem_limit_bytes=VMEM_LIMIT)


def _rms(x, g):
    return x * lax.rsqrt(jnp.mean(x * x, axis=-1, keepdims=True) + NORM_EPS) * g


def _silu(x):
    return x * jax.nn.sigmoid(x)


def _dot(a, b):
    return jnp.dot(a, b, preferred_element_type=F32)


def _dot_nt(a, b):
    return lax.dot_general(a, b, (((1,), (1,)), ((), ())), preferred_element_type=F32)


def _dot_tn(a, b):
    return lax.dot_general(a, b, (((0,), (0,)), ((), ())), preferred_element_type=F32)


def _modulated(x_ref, mod_ref, g_ref, k0, gi):
    m = mod_ref[0]
    y = _rms(x_ref[...], g_ref[gi:gi + 1, :])
    return (y * (1.0 + m[k0 + 1:k0 + 2, :]) + m[k0:k0 + 1, :]).astype(BF16)


def _ada_kernel(c_ref, w_ref, b_ref, o_ref):
    s = _silu(c_ref[...]).astype(BF16)
    o_ref[0] = _dot(s, w_ref[0].astype(BF16)) + b_ref[0]


def _ada(cs, ada_w, ada_b, tn=1024):
    depth, d, n = ada_w.shape
    r = cs.shape[0]
    return pl.pallas_call(
        _ada_kernel,
        grid=(depth, n // tn),
        in_specs=[
            pl.BlockSpec((r, d), lambda i, j: (0, 0)),
            pl.BlockSpec((1, d, tn), lambda i, j: (i, 0, j)),
            pl.BlockSpec((1, 1, tn), lambda i, j: (i, 0, j)),
        ],
        out_specs=pl.BlockSpec((1, r, tn), lambda i, j: (i, 0, j)),
        out_shape=jax.ShapeDtypeStruct((depth, r, n), F32),
        compiler_params=_cparams("parallel", "parallel"),
    )(cs, ada_w, ada_b.reshape(depth, 1, n))


def _with_tile_input(refs, k0, gi, chunk, body):
    x_ref, xn_ref, mod_ref, modn_ref, g_ref, h0_scr, h1_scr = refs
    i = pl.program_id(0)
    j = pl.program_id(1)
    tm = x_ref.shape[0]

    @pl.when(jnp.logical_and(i == 0, j == 0))
    def _():
        h0_scr[...] = _modulated(x_ref, mod_ref, g_ref, k0, gi)

    r0 = pl.multiple_of(jnp.minimum(j * chunk, tm - chunk), BF16_ROWS)

    def step(h_cur, h_next):
        h_next[pl.ds(r0, chunk), :] = _modulated(xn_ref.at[pl.ds(r0, chunk), :], modn_ref, g_ref, k0, gi)
        body(h_cur[...])

    @pl.when(i % 2 == 0)
    def _():
        step(h0_scr, h1_scr)

    @pl.when(i % 2 == 1)
    def _():
        step(h1_scr, h0_scr)


def _tile_input_specs(tm, d, rows_per_mod, g, n_tiles, n_steps, x_every_tile=True):
    chunk = -(-tm // (n_steps * BF16_ROWS)) * BF16_ROWS
    assert chunk <= tm and tm % BF16_ROWS == 0
    nxt = lambda i: jnp.minimum(i + 1, n_tiles - 1)
    x_spec = (pl.BlockSpec((tm, d), lambda i, j: (i, 0)) if x_every_tile else
              pl.BlockSpec((tm, d), lambda i, j: (0, 0), pipeline_mode=pl.Buffered(1)))
    specs = [
        x_spec,
        pl.BlockSpec((tm, d), lambda i, j: (nxt(i), 0)),
        pl.BlockSpec((1, N_MOD, d), lambda i, j: ((i * tm) // rows_per_mod, 0, 0)),
        pl.BlockSpec((1, N_MOD, d), lambda i, j: ((nxt(i) * tm) // rows_per_mod, 0, 0)),
        pl.BlockSpec(g.shape, lambda i, j: (0, 0)),
    ]
    return specs, [pltpu.VMEM((tm, d), BF16), pltpu.VMEM((tm, d), BF16)], chunk


def _ffn_kernel(x_ref, xn_ref, mod_ref, modn_ref, g_ref, wg_ref, wu_ref, wd_ref, o_ref,
                h0_scr, h1_scr, acc_scr, *, k0, gi, chunk):
    j = pl.program_id(1)

    @pl.when(j == 0)
    def _():
        acc_scr[...] = jnp.zeros_like(acc_scr)

    def body(h):
        gate = _dot(h, wg_ref[...])
        up = _dot(h, wu_ref[...])
        a = (_silu(gate) * up).astype(BF16)
        acc_scr[...] += _dot(a, wd_ref[...])

    _with_tile_input((x_ref, xn_ref, mod_ref, modn_ref, g_ref, h0_scr, h1_scr), k0, gi, chunk, body)

    @pl.when(j == pl.num_programs(1) - 1)
    def _():
        m = mod_ref[0]
        y = _rms(acc_scr[...], g_ref[gi + 1:gi + 2, :])
        o_ref[...] = x_ref[...] + (0.5 * m[k0 + 2:k0 + 3, :]) * y


def _ffn(x, mod, rows_per_mod, g, wg, wu, wd, *, layer, which, tm=512, tf=512):
    m_rows, d = x.shape
    f = wg.shape[-1]
    tm = min(tm, rows_per_mod)
    k0, gi = (0, 0) if which == 0 else (6, 4)
    n_tiles, n_steps = m_rows // tm, f // tf
    tile_specs, h_scratch, chunk = _tile_input_specs(tm, d, rows_per_mod, g, n_tiles, n_steps)
    return pl.pallas_call(
        functools.partial(_ffn_kernel, k0=k0, gi=gi, chunk=chunk),
        grid=(n_tiles, n_steps),
        in_specs=tile_specs + [
            pl.BlockSpec((None, None, d, tf), lambda i, j: (layer, which, 0, j)),
            pl.BlockSpec((None, None, d, tf), lambda i, j: (layer, which, 0, j)),
            pl.BlockSpec((None, None, tf, d), lambda i, j: (layer, which, j, 0)),
        ],
        out_specs=pl.BlockSpec((tm, d), lambda i, j: (i, 0)),
        out_shape=jax.ShapeDtypeStruct((m_rows, d), F32),
        scratch_shapes=h_scratch + [pltpu.VMEM((tm, d), F32)],
        compiler_params=_cparams("arbitrary", "arbitrary"),
    )(x, x, mod, mod, g, wg, wu, wd)


def _proj_plain_kernel(x_ref, xn_ref, mod_ref, modn_ref, g_ref, w_ref, o_ref, h0_scr, h1_scr,
                       *, k0, gi, chunk):
    def body(h):
        o_ref[...] = _dot(h, w_ref[...]).astype(o_ref.dtype)

    _with_tile_input((x_ref, xn_ref, mod_ref, modn_ref, g_ref, h0_scr, h1_scr), k0, gi, chunk, body)


def _proj_glu_kernel(x_ref, xn_ref, mod_ref, modn_ref, g_ref, wa_ref, wg_ref, ba_ref, bg_ref, o_ref,
                     h0_scr, h1_scr, *, k0, gi, chunk):
    def body(h):
        for c0 in range(0, o_ref.shape[1], GLU_COL_CHUNK):
            cols = slice(c0, c0 + GLU_COL_CHUNK)
            a = _dot(h, wa_ref[:, cols]) + ba_ref[:, cols]
            gate = _dot(h, wg_ref[:, cols]) + bg_ref[:, cols]
            o_ref[:, cols] = a * jax.nn.sigmoid(gate)

    _with_tile_input((x_ref, xn_ref, mod_ref, modn_ref, g_ref, h0_scr, h1_scr), k0, gi, chunk, body)


def _rope_groups(acc, cos_ref, sin_ref, scale, o_ref):
    n_groups = acc.shape[1] // LANE
    for gidx in range(n_groups):
        xg = acc[:, gidx * LANE:(gidx + 1) * LANE]
        t = (gidx % 2) * LANE
        rot = pltpu.roll(xg, LANE // 2, axis=1)
        o_ref[:, gidx * LANE:(gidx + 1) * LANE] = (
            xg * cos_ref[:, t:t + LANE] + rot * sin_ref[:, t:t + LANE]) * scale


def _proj_gla_kernel(x_ref, xn_ref, mod_ref, modn_ref, g_ref, w_ref, b_ref, cos_ref, sin_ref, o_ref,
                     h0_scr, h1_scr, *, k0, gi, chunk, q_scale, n_qk, n_v, rope):
    j = pl.program_id(1)

    def body(h):
        acc = _dot(h, w_ref[...])

        @pl.when(j < 2 * n_qk)
        def _():
            scale = jnp.where(j < n_qk, q_scale, 1.0).astype(F32)
            if rope:
                _rope_groups(acc, cos_ref, sin_ref, scale, o_ref)
            else:
                o_ref[...] = acc * scale

        @pl.when(jnp.logical_and(j >= 2 * n_qk, j < 2 * n_qk + n_v))
        def _():
            o_ref[...] = acc

        @pl.when(j >= 2 * n_qk + n_v)
        def _():
            o_ref[...] = _silu(acc + b_ref[...])

    _with_tile_input((x_ref, xn_ref, mod_ref, modn_ref, g_ref, h0_scr, h1_scr), k0, gi, chunk, body)


def _gla_gate_kernel(x_ref, mod_ref, g_ref, a1_ref, a2_ref, ba_ref, o_ref, *, k0, gi):
    h = _modulated(x_ref, mod_ref, g_ref, k0, gi)
    low = _dot(h, a1_ref[...]).astype(BF16)
    for d in range(2):
        z = _dot(low, a2_ref[d]) + ba_ref[d]
        log_sig = jnp.minimum(z, 0.0) - jnp.log1p(jnp.exp(-jnp.abs(z)))
        o_ref[d] = log_sig * (1.0 / GLA_GATE_NORMALIZER)


def _proj_call(kernel_fn, extra_specs, extra_args, x, mod, rows_per_mod, g, n, tn, out_dtype, tm, **static):
    m_rows, d = x.shape
    tm = min(tm, rows_per_mod)
    n_tiles, n_steps = m_rows // tm, n // tn
    tile_specs, h_scratch, chunk = _tile_input_specs(tm, d, rows_per_mod, g, n_tiles, n_steps,
                                                     x_every_tile=False)
    return pl.pallas_call(
        functools.partial(kernel_fn, chunk=chunk, **static),
        grid=(n_tiles, n_steps),
        in_specs=tile_specs + extra_specs(tm),
        out_specs=pl.BlockSpec((tm, tn), lambda i, j: (i, j)),
        out_shape=jax.ShapeDtypeStruct((m_rows, n), out_dtype),
        scratch_shapes=h_scratch,
        compiler_params=_cparams("arbitrary", "arbitrary"),
    )(x, x, mod, mod, g, *extra_args)


def _proj_plain(x, mod, rows_per_mod, g, w, *, k0, gi, out_dtype, tm=1024, tn=512):
    d, n = w.shape
    specs = lambda tm: [pl.BlockSpec((d, tn), lambda i, j: (0, j))]
    return _proj_call(_proj_plain_kernel, specs, (w,), x, mod, rows_per_mod, g, n, tn, out_dtype, tm,
                      k0=k0, gi=gi)


def _proj_glu(x, mod, rows_per_mod, g, w, b, *, k0, gi, tm=512, tn=2048):
    d = w.shape[0]
    n = w.shape[1] // 2
    nb = n // tn
    assert tn % GLU_COL_CHUNK == 0
    b2 = b.reshape(1, 2 * n)
    mode = dict(pipeline_mode=pl.Buffered(1)) if nb == 1 else {}
    specs = lambda tm: [
        pl.BlockSpec((d, tn), lambda i, j: (0, j), **mode),
        pl.BlockSpec((d, tn), lambda i, j: (0, j + nb), **mode),
        pl.BlockSpec((1, tn), lambda i, j: (0, j), **mode),
        pl.BlockSpec((1, tn), lambda i, j: (0, j + nb), **mode),
    ]
    return _proj_call(_proj_glu_kernel, specs, (w, w, b2, b2), x, mod, rows_per_mod, g, n, tn, F32, tm,
                      k0=k0, gi=gi)


def _proj_gla(x, mod, rows_per_mod, g, w_cat, b_r, cos, sin, *, k0, gi, dk_total, dv_total, rope,
              tm=1024, tn=512):
    d, n = w_cat.shape
    n_qk, n_v = dk_total // tn, dv_total // tn
    dk = dk_total // GLA_HEADS

    def specs(tm):
        seq_blocks = cos.shape[0] // tm
        tab_spec = pl.BlockSpec((tm, cos.shape[1]), lambda i, j: (i % seq_blocks, 0))
        return [
            pl.BlockSpec((d, tn), lambda i, j: (0, j)),
            pl.BlockSpec((1, tn), lambda i, j: (0, jnp.maximum(j - (2 * n_qk + n_v), 0))),
            tab_spec, tab_spec,
        ]

    return _proj_call(_proj_gla_kernel, specs, (w_cat, b_r.reshape(1, -1), cos, sin), x, mod, rows_per_mod, g,
                      n, tn, F32, tm, k0=k0, gi=gi, q_scale=dk ** -0.5, n_qk=n_qk, n_v=n_v, rope=rope)


def _gla_gate(x, mod, rows_per_mod, g, a1_cat, a2_pad, b_a, *, k0, gi, tm=256):
    m_rows, d = x.shape
    n = a2_pad.shape[2]
    tm = min(tm, rows_per_mod)
    return pl.pallas_call(
        functools.partial(_gla_gate_kernel, k0=k0, gi=gi),
        grid=(m_rows // tm,),
        in_specs=[
            pl.BlockSpec((tm, d), lambda i: (i, 0)),
            pl.BlockSpec((1, N_MOD, d), lambda i: ((i * tm) // rows_per_mod, 0, 0)),
            pl.BlockSpec(g.shape, lambda i: (0, 0)),
            pl.BlockSpec(a1_cat.shape, lambda i: (0, 0)),
            pl.BlockSpec(a2_pad.shape, lambda i: (0, 0, 0)),
            pl.BlockSpec((2, 1, n), lambda i: (0, 0, 0)),
        ],
        out_specs=pl.BlockSpec((2, tm, n), lambda i: (0, i, 0)),
        out_shape=jax.ShapeDtypeStruct((2, m_rows, n), F32),
        compiler_params=_cparams("parallel"),
    )(x, mod, g, a1_cat, a2_pad, b_a.reshape(2, 1, n))


def _gla_scan_kernel(q_ref, k_ref, v_ref, la_ref, s0_ref, o_ref, sf_ref, s_scr, *, n_sub, dk, dv):
    d = pl.program_id(1)
    c = pl.program_id(2)
    fwd = d == 0

    @pl.when(c == 0)
    def _():
        s_scr[...] = s0_ref[0, 0]

    ch = GLA_CHUNK
    row = lax.broadcasted_iota(jnp.int32, (ch, ch), 0)
    col = lax.broadcasted_iota(jnp.int32, (ch, ch), 1)
    tri = (row - col) * (1 - 2 * d) >= 0
    tri_f = tri.astype(F32)

    def sub(i, carry):
        r0 = pl.multiple_of(jnp.where(fwd, i, n_sub - 1 - i) * ch, ch)
        for h in range(GLA_HEADS):
            qc = q_ref[pl.ds(r0, ch), h * dk:(h + 1) * dk]
            kc = k_ref[pl.ds(r0, ch), h * dk:(h + 1) * dk]
            vc = v_ref[pl.ds(r0, ch), h * dv:(h + 1) * dv].astype(BF16)
            ac = la_ref[0, pl.ds(r0, ch), h * dk:(h + 1) * dk]
            cum = jnp.dot(tri_f, ac, precision=lax.Precision.HIGHEST, preferred_element_type=F32)
            total = jnp.sum(ac, axis=0, keepdims=True)
            qe = (qc * jnp.exp(cum)).astype(BF16)
            ke = (kc * jnp.exp(-cum)).astype(BF16)
            kd = (kc * jnp.exp(total - cum)).astype(BF16)
            scores = jnp.where(tri, _dot_nt(qe, ke), 0.0).astype(BF16)
            s_old = s_scr[h]
            o_ref[0, pl.ds(r0, ch), h * dv:(h + 1) * dv] = _dot(qe, s_old.astype(BF16)) + _dot(scores, vc)
            decay_col = jnp.transpose(jnp.broadcast_to(jnp.exp(total), (SUBLANE, dk)))[:, :1]
            s_scr[h] = decay_col * s_old + _dot_tn(kd, vc)
        return carry

    lax.fori_loop(0, n_sub, sub, 0)

    @pl.when(c == pl.num_programs(2) - 1)
    def _():
        sf_ref[0, 0] = s_scr[...]


def _gla_scan(p, la, s0, *, batch, seq, dk, dv, n_sub=4):
    h = GLA_HEADS
    n_sub = min(n_sub, seq // GLA_CHUNK)
    rb = n_sub * GLA_CHUNK
    nblk = seq // rb
    kq, kv = h * dk, h * dv
    blk = lambda b, d, c: b * nblk + c + d * (nblk - 1 - 2 * c)
    o, s_fin = pl.pallas_call(
        functools.partial(_gla_scan_kernel, n_sub=n_sub, dk=dk, dv=dv),
        grid=(batch, 2, nblk),
        in_specs=[
            pl.BlockSpec((rb, kq), lambda b, d, c: (blk(b, d, c), 0)),
            pl.BlockSpec((rb, kq), lambda b, d, c: (blk(b, d, c), 1)),
            pl.BlockSpec((rb, kv), lambda b, d, c: (blk(b, d, c), (2 * kq) // kv)),
            pl.BlockSpec((1, rb, kq), lambda b, d, c: (d, blk(b, d, c), 0)),
            pl.BlockSpec((1, 1, h, dk, dv), lambda b, d, c: (b, d, 0, 0, 0)),
        ],
        out_specs=[
            pl.BlockSpec((1, rb, kv), lambda b, d, c: (d, blk(b, d, c), 0)),
            pl.BlockSpec((1, 1, h, dk, dv), lambda b, d, c: (b, d, 0, 0, 0)),
        ],
        out_shape=[
            jax.ShapeDtypeStruct((2, batch * seq, kv), F32),
            jax.ShapeDtypeStruct((batch, 2, h, dk, dv), F32),
        ],
        scratch_shapes=[pltpu.VMEM((h, dk, dv), F32)],
        compiler_params=_cparams("parallel", "parallel", "arbitrary"),
    )(p, p, p, la, s0)
    return o, s_fin


def _na_kernel(q_ref, k_ref, v_ref, kc_ref, vc_ref, bias_ref, o_ref, *, rows, scale, plan):
    kc = kc_ref[...]
    vc = vc_ref[...]
    w = GRID_W
    n_q = NA_BLOCK_ROWS * w
    n_k = (NA_BLOCK_ROWS + NA_WIN_R) * w
    common, exceptions = plan

    def block(b, carry):
        r0 = b * NA_BLOCK_ROWS
        ws = jnp.clip(r0 - NA_WIN_R // 2, 0, rows - (NA_BLOCK_ROWS + NA_WIN_R))
        pid = common
        for blk, p in exceptions:
            pid = jnp.where(b == blk, p, pid)
        q0 = pl.multiple_of(r0 * w, w)
        k0 = pl.multiple_of(ws * w, w)
        q = q_ref[pl.ds(q0, n_q), :]
        kw = k_ref[pl.ds(k0, n_k), :]
        vw = v_ref[pl.ds(k0, n_k), :]
        s_lat = _dot_nt(q, kw) * scale + bias_ref[0, pid]
        s_ctx = _dot_nt(q, kc) * scale
        m = jnp.maximum(jnp.max(s_lat, axis=-1, keepdims=True), jnp.max(s_ctx, axis=-1, keepdims=True))
        p_lat = jnp.exp(s_lat - m)
        p_ctx = jnp.exp(s_ctx - m)
        denom = jnp.sum(p_lat, axis=-1, keepdims=True) + jnp.sum(p_ctx, axis=-1, keepdims=True)
        acc = _dot(p_lat.astype(BF16), vw) + _dot(p_ctx.astype(BF16), vc)
        o_ref[pl.ds(q0, n_q), :] = (acc / denom).astype(o_ref.dtype)
        return carry

    lax.fori_loop(0, rows // NA_BLOCK_ROWS, block, 0, unroll=NA_BLOCK_UNROLL)


def _na_plan(rows):
    kw_rows = NA_BLOCK_ROWS + NA_WIN_R
    patterns, ids = [], []
    for b in range(rows // NA_BLOCK_ROWS):
        r0 = b * NA_BLOCK_ROWS
        ws = min(max(r0 - NA_WIN_R // 2, 0), rows - kw_rows)
        key = []
        for rho in range(NA_BLOCK_ROWS):
            r = r0 + rho
            rs = min(max(r - NA_WIN_R // 2, 0), rows - NA_WIN_R)
            assert ws <= rs and rs + NA_WIN_R <= ws + kw_rows
            key.append((rs - ws, ws - r + NA_WIN_R - 1))
        key = tuple(key)
        if key not in patterns:
            patterns.append(key)
        ids.append(patterns.index(key))
    common = max(set(ids), key=ids.count)
    exceptions = tuple((b, p) for b, p in enumerate(ids) if p != common)
    dr = [[[min(max(base + om, 0), 2 * NA_WIN_R - 2) for om in range(kw_rows)] for (_, base) in pat]
          for pat in patterns]
    valid = [[[first <= om < first + NA_WIN_R for om in range(kw_rows)] for (first, _) in pat]
             for pat in patterns]
    return dr, valid, (common, exceptions)


def _na_bias_table(rpb, rows):
    w = GRID_W
    dr, valid_rows, plan = _na_plan(rows)
    dr, valid_rows = np.asarray(dr), np.asarray(valid_rows)
    col = np.arange(w)
    cs = np.clip(col - NA_WIN_C // 2, 0, w - NA_WIN_C)
    valid_cols = (col[None, :] >= cs[:, None]) & (col[None, :] < cs[:, None] + NA_WIN_C)
    rel = np.clip(col[None, :] - col[:, None] + (NA_WIN_C - 1), 0, 2 * NA_WIN_C - 2)
    n_rel = 2 * NA_WIN_C - 1
    flat = dr[:, :, None, :, None] * n_rel + rel[None, None, :, None, :]
    valid = valid_rows[:, :, None, :, None] & valid_cols[None, None, :, None, :]
    n_heads = rpb.shape[0]
    t = jnp.where(jnp.asarray(valid)[None], rpb.reshape(n_heads, -1)[:, jnp.asarray(flat, jnp.int32)], MASK_BIAS)
    t = t.reshape(n_heads, dr.shape[0], NA_BLOCK_ROWS * w, (NA_BLOCK_ROWS + NA_WIN_R) * w)
    return t.astype(F32), plan


def _na_attention(qkv, qkv_ctx, bias, plan, *, batch, seq, ctx_len, d):
    hd = d // NA_HEADS
    rows = seq // GRID_W
    assert rows >= NA_BLOCK_ROWS + NA_WIN_R and rows % NA_BLOCK_ROWS == 0 and hd == LANE
    nh = NA_HEADS
    return pl.pallas_call(
        functools.partial(_na_kernel, rows=rows, scale=hd ** -0.5, plan=plan),
        grid=(batch, nh),
        in_specs=[
            pl.BlockSpec((seq, hd), lambda b, h: (b, h)),
            pl.BlockSpec((seq, hd), lambda b, h: (b, nh + h)),
            pl.BlockSpec((seq, hd), lambda b, h: (b, 2 * nh + h)),
            pl.BlockSpec((ctx_len, hd), lambda b, h: (b, nh + h)),
            pl.BlockSpec((ctx_len, hd), lambda b, h: (b, 2 * nh + h)),
            pl.BlockSpec((1,) + bias.shape[1:], lambda b, h: (h, 0, 0, 0)),
        ],
        out_specs=pl.BlockSpec((seq, hd), lambda b, h: (b, h)),
        out_shape=jax.ShapeDtypeStruct((batch * seq, d), BF16),
        compiler_params=_cparams("parallel", "parallel"),
    )(qkv, qkv, qkv, qkv_ctx, qkv_ctx, bias)


def _finish(a, w_ref, b_ref, x_ref, mod_ref, g_ref, o_ref):
    y = _dot(a, w_ref[...])
    if b_ref is not None:
        y = y + b_ref[...]
    m = mod_ref[0]
    o_ref[...] = x_ref[...] + m[5:6, :] * _rms(y, g_ref[3:4, :])


def _out_na_kernel(a_ref, w_ref, x_ref, mod_ref, g_ref, o_ref):
    _finish(a_ref[...], w_ref, None, x_ref, mod_ref, g_ref, o_ref)


def _out_gla_kernel(of_ref, ob_ref, r_ref, ng_ref, w_ref, x_ref, mod_ref, g_ref, o_ref, a_scr, *, dv):
    ng = ng_ref[...]
    for h in range(GLA_HEADS):
        sl = slice(h * dv, (h + 1) * dv)
        o = of_ref[0, :, sl] + ob_ref[0, :, sl]
        a_scr[:, sl] = (_rms(o, ng) * r_ref[:, sl]).astype(BF16)
    _finish(a_scr[...], w_ref, None, x_ref, mod_ref, g_ref, o_ref)


def _out_conv_kernel(prev_ref, cur_ref, next_ref, wdw_ref, bdw_ref, lng_ref, lnb_ref, w_ref, b_ref,
                     x_ref, mod_ref, g_ref, o_ref, ext_scr, u_scr, *, seq, width, rb, ls):
    tm = cur_ref.shape[0]
    pos = (pl.program_id(0) * tm) % seq
    halo = CONV_HALO
    ext_scr[0:halo, :] = jnp.where(pos > 0, prev_ref[...], 0.0)
    ext_scr[halo:halo + tm, :] = cur_ref[...]
    ext_scr[halo + tm:, :] = jnp.where(pos + tm < seq, next_ref[...], 0.0)
    off = halo - width // 2
    d = cur_ref.shape[1]
    n_q = (off + width - 1) // SUBLANE + 1

    def rows(blk, carry):
        r0 = pl.multiple_of(blk * rb, rb)
        for c0 in range(0, d, ls):
            v = ext_scr[pl.ds(r0, rb + 2 * halo), c0:c0 + ls]
            acc = jnp.broadcast_to(bdw_ref[:, c0:c0 + ls], (rb, ls))
            for s in range(SUBLANE):
                vs = v[s:s + rb + SUBLANE * (n_q - 1)]
                for q in range(n_q):
                    k = SUBLANE * q + s - off
                    if 0 <= k < width:
                        wk = jnp.concatenate([wdw_ref[k, :, c0:c0 + ls]] * (rb // SUBLANE), axis=0)
                        acc = acc + wk * vs[SUBLANE * q:SUBLANE * q + rb]
            u_scr[pl.ds(r0, rb), c0:c0 + ls] = acc
        return carry

    lax.fori_loop(0, tm // rb, rows, 0)
    u = u_scr[...]
    mu = jnp.mean(u, axis=-1, keepdims=True)
    uc = u - mu
    var = jnp.mean(uc * uc, axis=-1, keepdims=True)
    y = uc * lax.rsqrt(var + NORM_EPS) * lng_ref[...] + lnb_ref[...]
    _finish(_silu(y).astype(BF16), w_ref, b_ref, x_ref, mod_ref, g_ref, o_ref)


def _out_tail_specs(tm, d, rows_per_mod, g):
    return [
        pl.BlockSpec((d, d), lambda i: (0, 0)),
        pl.BlockSpec((tm, d), lambda i: (i, 0)),
        pl.BlockSpec((1, N_MOD, d), lambda i: ((i * tm) // rows_per_mod, 0, 0)),
        pl.BlockSpec(g.shape, lambda i: (0, 0)),
    ]


def _out_call(kernel, head_specs, head_args, w, x, mod, rows_per_mod, g, tm, scratch=()):
    m_rows, d = x.shape
    tail = _out_tail_specs(tm, d, rows_per_mod, g)
    return pl.pallas_call(
        kernel,
        grid=(m_rows // tm,),
        in_specs=head_specs + tail,
        out_specs=pl.BlockSpec((tm, d), lambda i: (i, 0)),
        out_shape=jax.ShapeDtypeStruct((m_rows, d), F32),
        scratch_shapes=list(scratch),
        compiler_params=_cparams("parallel"),
    )(*head_args, w, x, mod, g)


def _out_na(a, w, x, mod, rows_per_mod, g, tm=512):
    d = x.shape[1]
    tm = min(tm, rows_per_mod)
    return _out_call(_out_na_kernel, [pl.BlockSpec((tm, d), lambda i: (i, 0))], [a],
                     w, x, mod, rows_per_mod, g, tm)


def _out_gla(o, p, norm_g, w, x, mod, rows_per_mod, g, *, dv, tm=256):
    d = x.shape[1]
    tm = min(tm, rows_per_mod)
    r_blk = (p.shape[1] - d) // d
    specs = [
        pl.BlockSpec((1, tm, d), lambda i: (0, i, 0)),
        pl.BlockSpec((1, tm, d), lambda i: (1, i, 0)),
        pl.BlockSpec((tm, d), lambda i: (i, r_blk)),
        pl.BlockSpec((1, dv), lambda i: (0, 0)),
    ]
    return _out_call(functools.partial(_out_gla_kernel, dv=dv), specs, [o, o, p, norm_g.reshape(1, dv)],
                     w, x, mod, rows_per_mod, g, tm, scratch=[pltpu.VMEM((tm, d), BF16)])


def _out_conv(u, w_dw, b_dw, ln_g, ln_b, w, b, x, mod, rows_per_mod, g, tm=256,
              rb=CONV_ROW_BLOCK, ls=CONV_LANE_BLOCK):
    m_rows, d = x.shape
    tm = min(tm, rows_per_mod)
    width = w_dw.shape[0]
    assert width // 2 < CONV_HALO and tm % CONV_HALO == 0 and tm % rb == 0 and d % ls == 0
    hb = tm // CONV_HALO
    last = m_rows // CONV_HALO - 1
    row = lambda v: v.reshape(1, d)
    specs = [
        pl.BlockSpec((CONV_HALO, d), lambda i: (jnp.maximum(i * hb - 1, 0), 0)),
        pl.BlockSpec((tm, d), lambda i: (i, 0)),
        pl.BlockSpec((CONV_HALO, d), lambda i: (jnp.minimum((i + 1) * hb, last), 0)),
        pl.BlockSpec((width, SUBLANE, d), lambda i: (0, 0, 0)),
    ] + [pl.BlockSpec((1, d), lambda i: (0, 0))] * 3
    kernel = functools.partial(_out_conv_kernel, seq=rows_per_mod, width=width, rb=rb, ls=ls)
    tail = _out_tail_specs(tm, d, rows_per_mod, g)
    return pl.pallas_call(
        kernel,
        grid=(m_rows // tm,),
        in_specs=specs + [tail[0], pl.BlockSpec((1, d), lambda i: (0, 0))] + tail[1:],
        out_specs=pl.BlockSpec((tm, d), lambda i: (i, 0)),
        out_shape=jax.ShapeDtypeStruct((m_rows, d), F32),
        scratch_shapes=[pltpu.VMEM((tm + 2 * CONV_HALO, d), F32), pltpu.VMEM((tm, d), F32)],
        compiler_params=_cparams("parallel"),
    )(u, u, u, jnp.broadcast_to(w_dw[:, None, :], (width, SUBLANE, d)),
      row(b_dw), row(ln_g), row(ln_b), w, row(b), x, mod, g)


def _rope_tables(seq, dk):
    n = dk // 2
    inv = ROPE_THETA ** (-jnp.arange(0, n, 2, dtype=F32) / n)
    t = jnp.arange(seq)

    def half(pos):
        ang = pos.astype(F32)[:, None] * inv
        c, s = jnp.cos(ang), jnp.sin(ang)
        return jnp.concatenate([c, c], axis=-1), jnp.concatenate([-s, s], axis=-1)

    cr, sr = half(t // GRID_W)
    cc, sc = half(t % GRID_W)
    return jnp.concatenate([cr, cc], axis=-1), jnp.concatenate([sr, sc], axis=-1)


def kernel(x, c, ctx, c_ctx, ada_w, ada_b, norm_g, ffn_w_gate, ffn_w_up, ffn_w_down, conv_w_in, conv_b_in, conv_w_dw, conv_b_dw, conv_ln_g, conv_ln_b, conv_w_out, conv_b_out, gla_w_q, gla_w_k, gla_w_v, gla_w_r, gla_b_r, gla_w_a1, gla_w_a2, gla_b_a, gla_norm_g, gla_w_o, na_w_qkv, na_rpb, na_w_o):
    batch, seq, d = x.shape
    ctx_len = ctx.shape[1]
    depth = ada_w.shape[0]
    dk_total = gla_w_q.shape[-1]
    dv_total = gla_w_v.shape[-1]
    dk, dv = dk_total // GLA_HEADS, dv_total // GLA_HEADS

    pad = (-(batch + 1)) % 8
    cs = jnp.concatenate([c, c_ctx[None, :], jnp.zeros((pad, d), F32)], axis=0)
    mods = _ada(cs, ada_w, ada_b).reshape(depth, batch + 1 + pad, N_MOD, d)

    wg_all, wu_all, wd_all = (w.astype(BF16) for w in (ffn_w_gate, ffn_w_up, ffn_w_down))

    x_lat = x.reshape(batch * seq, d)
    x_ctx = ctx.reshape(batch * ctx_len, d)
    reads_ctx = [(i % N_MIXERS) != 0 for i in range(depth)]

    for i in range(depth):
        kind, j = i % N_MIXERS, i // N_MIXERS
        ctx_out = any(reads_ctx[i + 1:])
        ctx_in = ctx_out or reads_ctx[i]
        mod_lat = mods[i, :batch]
        mod_ctx = jnp.broadcast_to(mods[i, batch], (batch, N_MOD, d))
        g = norm_g[i]
        lat = dict(mod=mod_lat, rows_per_mod=seq, g=g)
        cx = dict(mod=mod_ctx, rows_per_mod=ctx_len, g=g)

        ffn1 = dict(wg=wg_all, wu=wu_all, wd=wd_all, layer=i, which=0)
        ffn2 = dict(wg=wg_all, wu=wu_all, wd=wd_all, layer=i, which=1)

        cx_ffn = dict(mod=mod_ctx[:1], rows_per_mod=batch * ctx_len, g=g)
        x_lat = _ffn(x_lat, **lat, **ffn1)
        if ctx_in:
            x_ctx = _ffn(x_ctx, **cx_ffn, **ffn1)

        x_ctx_mixed = None
        if kind == 0:
            w_in = conv_w_in[j].astype(BF16)
            w_out = conv_w_out[j].astype(BF16)
            tail = (conv_w_dw[j], conv_b_dw[j], conv_ln_g[j], conv_ln_b[j], w_out, conv_b_out[j])
            u = _proj_glu(x_lat, w=w_in, b=conv_b_in[j], k0=3, gi=2, **lat)
            x_lat = _out_conv(u, *tail, x_lat, mod_lat, seq, g)
            if ctx_out:
                uc = _proj_glu(x_ctx, w=w_in, b=conv_b_in[j], k0=3, gi=2, **cx)
                x_ctx_mixed = _out_conv(uc, *tail, x_ctx, mod_ctx, ctx_len, g)
        elif kind == 1:
            w_cat = jnp.concatenate([gla_w_q[j], gla_w_k[j], gla_w_v[j], gla_w_r[j]], axis=1).astype(BF16)
            a1_cat = jnp.concatenate(
                [gla_w_a1[j, 0], gla_w_a1[j, 1], jnp.zeros((d, LANE - 2 * GLA_GATE_RANK), F32)], axis=1).astype(BF16)
            a2_pad = jnp.zeros((2, LANE, dk_total), F32)
            for dd in range(2):
                a2_pad = a2_pad.at[dd, dd * GLA_GATE_RANK:(dd + 1) * GLA_GATE_RANK].set(gla_w_a2[j, dd])
            a2_pad = a2_pad.astype(BF16)
            w_o = gla_w_o[j].astype(BF16)
            cos, sin = _rope_tables(seq, dk)
            proj = dict(w_cat=w_cat, b_r=gla_b_r[j], k0=3, gi=2, dk_total=dk_total, dv_total=dv_total)
            gate = dict(a1_cat=a1_cat, a2_pad=a2_pad, b_a=gla_b_a[j], k0=3, gi=2)

            p_ctx = _proj_gla(x_ctx, cos=cos, sin=sin, rope=False, **cx, **proj)
            la_ctx = _gla_gate(x_ctx, **cx, **gate)
            s0 = jnp.zeros((batch, 2, GLA_HEADS, dk, dv), F32)
            o_ctx, s_ctx = _gla_scan(p_ctx, la_ctx, s0, batch=batch, seq=ctx_len, dk=dk, dv=dv)

            p_lat = _proj_gla(x_lat, cos=cos, sin=sin, rope=True, **lat, **proj)
            la_lat = _gla_gate(x_lat, **lat, **gate)
            o_lat, _ = _gla_scan(p_lat, la_lat, s_ctx, batch=batch, seq=seq, dk=dk, dv=dv)

            x_lat = _out_gla(o_lat, p_lat, gla_norm_g[j], w_o, x_lat, mod_lat, seq, g, dv=dv)
            if ctx_out:
                x_ctx_mixed = _out_gla(o_ctx, p_ctx, gla_norm_g[j], w_o, x_ctx, mod_ctx, ctx_len, g, dv=dv)
        else:
            assert not ctx_out, "context self-attention output is not needed for this depth"
            w_qkv = na_w_qkv[j].astype(BF16)
            qkv = _proj_plain(x_lat, w=w_qkv, k0=3, gi=2, out_dtype=BF16, **lat)
            qkv_ctx = _proj_plain(x_ctx, w=w_qkv, k0=3, gi=2, out_dtype=BF16, **cx)
            bias, plan = _na_bias_table(na_rpb[j], seq // GRID_W)
            a = _na_attention(qkv, qkv_ctx, bias, plan, batch=batch, seq=seq, ctx_len=ctx_len, d=d)
            x_lat = _out_na(a, na_w_o[j].astype(BF16), x_lat, mod_lat, seq, g)

        x_lat = _ffn(x_lat, **lat, **ffn2)
        if ctx_out:
            x_ctx = _ffn(x_ctx_mixed, **cx_ffn, **ffn2)

    return x_lat.reshape(batch, seq, d)
```

```python
import functools
import math

import jax
import jax.numpy as jnp
from jax import lax
from jax.experimental import pallas as pl
from jax.experimental.pallas import tpu as pltpu

F32 = jnp.float32
BF16 = jnp.bfloat16

N_MIXERS = 3
N_MOD = 9
NORM_EPS = 1e-6
GRID_W = 64
GLA_HEADS = 4
GLA_GATE_RANK = 16
GLA_GATE_NORMALIZER = 16.0
GLA_CHUNK = 64
ROPE_THETA = 10000.0
NA_HEADS = 16
NA_WIN_R = 8
NA_WIN_C = 16
MASK_BIAS = -1e30
NA_BLOCK_ROWS = 4
NA_BLOCK_UNROLL = 4

LANE = 128
SUBLANE = 8
BF16_ROWS = 16
CONV_HALO = 16
CONV_ROW_BLOCK = 64
CONV_LANE_BLOCK = 128
GLU_COL_CHUNK = 512
VMEM_LIMIT = 56 * 1024 * 1024


def _cparams(*sem):
    return pltpu.CompilerParams(dimension_semantics=sem, vmem_limit_bytes=VMEM_LIMIT)


def _rms(x, g):
    return x * lax.rsqrt(jnp.mean(x * x, axis=-1, keepdims=True) + NORM_EPS) * g


def _silu(x):
    return x * jax.nn.sigmoid(x)


def _dot(a, b):
    return jnp.dot(a, b, preferred_element_type=F32)


def _dot_nt(a, b):
    return lax.dot_general(a, b, (((1,), (1,)), ((), ())), preferred_element_type=F32)


def _dot_tn(a, b):
    return lax.dot_general(a, b, (((0,), (0,)), ((), ())), preferred_element_type=F32)


def _modulated(x_ref, mod_ref, g_ref, k0, gi):
    m = mod_ref[0]
    y = _rms(x_ref[...], g_ref[gi:gi + 1, :])
    return (y * (1.0 + m[k0 + 1:k0 + 2, :]) + m[k0:k0 + 1, :]).astype(BF16)


def _ada_kernel(c_ref, w_ref, b_ref, o_ref):
    s = _silu(c_ref[...]).astype(BF16)
    o_ref[0] = _dot(s, w_ref[0].astype(BF16)) + b_ref[0]


def _ada(cs, ada_w, ada_b, tn=1024):
    depth, d, n = ada_w.shape
    r = cs.shape[0]
    return pl.pallas_call(
        _ada_kernel,
        grid=(depth, n // tn),
        in_specs=[
            pl.BlockSpec((r, d), lambda i, j: (0, 0)),
            pl.BlockSpec((1, d, tn), lambda i, j: (i, 0, j)),
            pl.BlockSpec((1, 1, tn), lambda i, j: (i, 0, j)),
        ],
        out_specs=pl.BlockSpec((1, r, tn), lambda i, j: (i, 0, j)),
        out_shape=jax.ShapeDtypeStruct((depth, r, n), F32),
        compiler_params=_cparams("parallel", "parallel"),
    )(cs, ada_w, ada_b.reshape(depth, 1, n))


def _with_tile_input(refs, k0, gi, chunk, body):
    x_ref, xn_ref, mod_ref, modn_ref, g_ref, h0_scr, h1_scr = refs
    i = pl.program_id(0)
    j = pl.program_id(1)
    tm = x_ref.shape[0]

    @pl.when(jnp.logical_and(i == 0, j == 0))
    def _():
        h0_scr[...] = _modulated(x_ref, mod_ref, g_ref, k0, gi)

    r0 = pl.multiple_of(jnp.minimum(j * chunk, tm - chunk), BF16_ROWS)

    def step(h_cur, h_next):
        h_next[pl.ds(r0, chunk), :] = _modulated(xn_ref.at[pl.ds(r0, chunk), :], modn_ref, g_ref, k0, gi)
        body(h_cur[...])

    @pl.when(i % 2 == 0)
    def _():
        step(h0_scr, h1_scr)

    @pl.when(i % 2 == 1)
    def _():
        step(h1_scr, h0_scr)


def _tile_input_specs(tm, d, rows_per_mod, g, n_tiles, n_steps, x_every_tile=True):
    chunk = -(-tm // (n_steps * BF16_ROWS)) * BF16_ROWS
    assert chunk <= tm and tm % BF16_ROWS == 0
    nxt = lambda i: jnp.minimum(i + 1, n_tiles - 1)
    x_spec = (pl.BlockSpec((tm, d), lambda i, j: (i, 0)) if x_every_tile else
              pl.BlockSpec((tm, d), lambda i, j: (0, 0), pipeline_mode=pl.Buffered(1)))
    specs = [
        x_spec,
        pl.BlockSpec((tm, d), lambda i, j: (nxt(i), 0)),
        pl.BlockSpec((1, N_MOD, d), lambda i, j: ((i * tm) // rows_per_mod, 0, 0)),
        pl.BlockSpec((1, N_MOD, d), lambda i, j: ((nxt(i) * tm) // rows_per_mod, 0, 0)),
        pl.BlockSpec(g.shape, lambda i, j: (0, 0)),
    ]
    return specs, [pltpu.VMEM((tm, d), BF16), pltpu.VMEM((tm, d), BF16)], chunk


def _ffn_kernel(x_ref, xn_ref, mod_ref, modn_ref, g_ref, wg_ref, wu_ref, wd_ref, o_ref,
                h0_scr, h1_scr, acc_scr, *, k0, gi, chunk):
    j = pl.program_id(1)

    @pl.when(j == 0)
    def _():
        acc_scr[...] = jnp.zeros_like(acc_scr)

    def body(h):
        gate = _dot(h, wg_ref[...])
        up = _dot(h, wu_ref[...])
        a = (_silu(gate) * up).astype(BF16)
        acc_scr[...] += _dot(a, wd_ref[...])

    _with_tile_input((x_ref, xn_ref, mod_ref, modn_ref, g_ref, h0_scr, h1_scr), k0, gi, chunk, body)

    @pl.when(j == pl.num_programs(1) - 1)
    def _():
        m = mod_ref[0]
        y = _rms(acc_scr[...], g_ref[gi + 1:gi + 2, :])
        o_ref[...] = x_ref[...] + (0.5 * m[k0 + 2:k0 + 3, :]) * y


def _ffn(x, mod, rows_per_mod, g, wg, wu, wd, *, layer, which, tm=512, tf=512):
    m_rows, d = x.shape
    f = wg.shape[-1]
    tm = min(tm, rows_per_mod)
    k0, gi = (0, 0) if which == 0 else (6, 4)
    n_tiles, n_steps = m_rows // tm, f // tf
    tile_specs, h_scratch, chunk = _tile_input_specs(tm, d, rows_per_mod, g, n_tiles, n_steps)
    return pl.pallas_call(
        functools.partial(_ffn_kernel, k0=k0, gi=gi, chunk=chunk),
        grid=(n_tiles, n_steps),
        in_specs=tile_specs + [
            pl.BlockSpec((None, None, d, tf), lambda i, j: (layer, which, 0, j)),
            pl.BlockSpec((None, None, d, tf), lambda i, j: (layer, which, 0, j)),
            pl.BlockSpec((None, None, tf, d), lambda i, j: (layer, which, j, 0)),
        ],
        out_specs=pl.BlockSpec((tm, d), lambda i, j: (i, 0)),
        out_shape=jax.ShapeDtypeStruct((m_rows, d), F32),
        scratch_shapes=h_scratch + [pltpu.VMEM((tm, d), F32)],
        compiler_params=_cparams("arbitrary", "arbitrary"),
    )(x, x, mod, mod, g, wg, wu, wd)


def _proj_plain_kernel(x_ref, xn_ref, mod_ref, modn_ref, g_ref, w_ref, o_ref, h0_scr, h1_scr,
                       *, k0, gi, chunk):
    def body(h):
        o_ref[...] = _dot(h, w_ref[...]).astype(o_ref.dtype)

    _with_tile_input((x_ref, xn_ref, mod_ref, modn_ref, g_ref, h0_scr, h1_scr), k0, gi, chunk, body)


def _proj_glu_kernel(x_ref, xn_ref, mod_ref, modn_ref, g_ref, wa_ref, wg_ref, ba_ref, bg_ref, o_ref,
                     h0_scr, h1_scr, *, k0, gi, chunk):
    def body(h):
        for c0 in range(0, o_ref.shape[1], GLU_COL_CHUNK):
            cols = slice(c0, c0 + GLU_COL_CHUNK)
            a = _dot(h, wa_ref[:, cols]) + ba_ref[:, cols]
            gate = _dot(h, wg_ref[:, cols]) + bg_ref[:, cols]
            o_ref[:, cols] = a * jax.nn.sigmoid(gate)

    _with_tile_input((x_ref, xn_ref, mod_ref, modn_ref, g_ref, h0_scr, h1_scr), k0, gi, chunk, body)


def _rope_groups(acc, cos_ref, sin_ref, scale, o_ref):
    n_groups = acc.shape[1] // LANE
    for gidx in range(n_groups):
        xg = acc[:, gidx * LANE:(gidx + 1) * LANE]
        t = (gidx % 2) * LANE
        rot = pltpu.roll(xg, LANE // 2, axis=1)
        o_ref[:, gidx * LANE:(gidx + 1) * LANE] = (
            xg * cos_ref[:, t:t + LANE] + rot * sin_ref[:, t:t + LANE]) * scale


def _proj_gla_kernel(x_ref, xn_ref, mod_ref, modn_ref, g_ref, w_ref, b_ref, cos_ref, sin_ref, o_ref,
                     h0_scr, h1_scr, *, k0, gi, chunk, q_scale, n_qk, n_v, rope):
    j = pl.program_id(1)

    def body(h):
        acc = _dot(h, w_ref[...])

        @pl.when(j < 2 * n_qk)
        def _():
            scale = jnp.where(j < n_qk, q_scale, 1.0).astype(F32)
            if rope:
                _rope_groups(acc, cos_ref, sin_ref, scale, o_ref)
            else:
                o_ref[...] = acc * scale

        @pl.when(jnp.logical_and(j >= 2 * n_qk, j < 2 * n_qk + n_v))
        def _():
            o_ref[...] = acc

        @pl.when(j >= 2 * n_qk + n_v)
        def _():
            o_ref[...] = _silu(acc + b_ref[...])

    _with_tile_input((x_ref, xn_ref, mod_ref, modn_ref, g_ref, h0_scr, h1_scr), k0, gi, chunk, body)


def _gla_gate_kernel(x_ref, mod_ref, g_ref, a1_ref, a2_ref, ba_ref, o_ref, *, k0, gi):
    h = _modulated(x_ref, mod_ref, g_ref, k0, gi)
    low = _dot(h, a1_ref[...]).astype(BF16)
    for d in range(2):
        z = _dot(low, a2_ref[d]) + ba_ref[d]
        log_sig = jnp.minimum(z, 0.0) - jnp.log1p(jnp.exp(-jnp.abs(z)))
        o_ref[d] = log_sig * (1.0 / GLA_GATE_NORMALIZER)


def _proj_call(kernel_fn, extra_specs, extra_args, x, mod, rows_per_mod, g, n, tn, out_dtype, tm, **static):
    m_rows, d = x.shape
    tm = min(tm, rows_per_mod)
    n_tiles, n_steps = m_rows // tm, n // tn
    tile_specs, h_scratch, chunk = _tile_input_specs(tm, d, rows_per_mod, g, n_tiles, n_steps,
                                                     x_every_tile=False)
    return pl.pallas_call(
        functools.partial(kernel_fn, chunk=chunk, **static),
        grid=(n_tiles, n_steps),
        in_specs=tile_specs + extra_specs(tm),
        out_specs=pl.BlockSpec((tm, tn), lambda i, j: (i, j)),
        out_shape=jax.ShapeDtypeStruct((m_rows, n), out_dtype),
        scratch_shapes=h_scratch,
        compiler_params=_cparams("arbitrary", "arbitrary"),
    )(x, x, mod, mod, g, *extra_args)


def _proj_plain(x, mod, rows_per_mod, g, w, *, k0, gi, out_dtype, tm=1024, tn=512):
    d, n = w.shape
    specs = lambda tm: [pl.BlockSpec((d, tn), lambda i, j: (0, j))]
    return _proj_call(_proj_plain_kernel, specs, (w,), x, mod, rows_per_mod, g, n, tn, out_dtype, tm,
                      k0=k0, gi=gi)


def _proj_glu(x, mod, rows_per_mod, g, w, b, *, k0, gi, tm=512, tn=2048):
    d = w.shape[0]
    n = w.shape[1] // 2
    nb = n // tn
    assert tn % GLU_COL_CHUNK == 0
    b2 = b.reshape(1, 2 * n)
    mode = dict(pipeline_mode=pl.Buffered(1)) if nb == 1 else {}
    specs = lambda tm: [
        pl.BlockSpec((d, tn), lambda i, j: (0, j), **mode),
        pl.BlockSpec((d, tn), lambda i, j: (0, j + nb), **mode),
        pl.BlockSpec((1, tn), lambda i, j: (0, j), **mode),
        pl.BlockSpec((1, tn), lambda i, j: (0, j + nb), **mode),
    ]
    return _proj_call(_proj_glu_kernel, specs, (w, w, b2, b2), x, mod, rows_per_mod, g, n, tn, F32, tm,
                      k0=k0, gi=gi)


def _proj_gla(x, mod, rows_per_mod, g, w_cat, b_r, cos, sin, *, k0, gi, dk_total, dv_total, rope,
              tm=1024, tn=512):
    d, n = w_cat.shape
    n_qk, n_v = dk_total // tn, dv_total // tn
    dk = dk_total // GLA_HEADS

    def specs(tm):
        seq_blocks = cos.shape[0] // tm
        tab_spec = pl.BlockSpec((tm, cos.shape[1]), lambda i, j: (i % seq_blocks, 0))
        return [
            pl.BlockSpec((d, tn), lambda i, j: (0, j)),
            pl.BlockSpec((1, tn), lambda i, j: (0, jnp.maximum(j - (2 * n_qk + n_v), 0))),
            tab_spec, tab_spec,
        ]

    return _proj_call(_proj_gla_kernel, specs, (w_cat, b_r.reshape(1, -1), cos, sin), x, mod, rows_per_mod, g,
                      n, tn, F32, tm, k0=k0, gi=gi, q_scale=dk ** -0.5, n_qk=n_qk, n_v=n_v, rope=rope)


def _gla_gate(x, mod, rows_per_mod, g, a1_cat, a2_pad, b_a, *, k0, gi, tm=256):
    m_rows, d = x.shape
    n = a2_pad.shape[2]
    tm = min(tm, rows_per_mod)
    return pl.pallas_call(
        functools.partial(_gla_gate_kernel, k0=k0, gi=gi),
        grid=(m_rows // tm,),
        in_specs=[
            pl.BlockSpec((tm, d), lambda i: (i, 0)),
            pl.BlockSpec((1, N_MOD, d), lambda i: ((i * tm) // rows_per_mod, 0, 0)),
            pl.BlockSpec(g.shape, lambda i: (0, 0)),
            pl.BlockSpec(a1_cat.shape, lambda i: (0, 0)),
            pl.BlockSpec(a2_pad.shape, lambda i: (0, 0, 0)),
            pl.BlockSpec((2, 1, n), lambda i: (0, 0, 0)),
        ],
        out_specs=pl.BlockSpec((2, tm, n), lambda i: (0, i, 0)),
        out_shape=jax.ShapeDtypeStruct((2, m_rows, n), F32),
        compiler_params=_cparams("parallel"),
    )(x, mod, g, a1_cat, a2_pad, b_a.reshape(2, 1, n))


def _gla_scan_kernel(q_ref, k_ref, v_ref, la_ref, s0_ref, o_ref, sf_ref, s_scr, *, n_sub, dk, dv):
    d = pl.program_id(1)
    c = pl.program_id(2)
    fwd = d == 0

    @pl.when(c == 0)
    def _():
        s_scr[...] = s0_ref[0, 0]

    ch = GLA_CHUNK
    row = lax.broadcasted_iota(jnp.int32, (ch, ch), 0)
    col = lax.broadcasted_iota(jnp.int32, (ch, ch), 1)
    tri = (row - col) * (1 - 2 * d) >= 0
    tri_f = tri.astype(F32)

    def sub(i, carry):
        r0 = pl.multiple_of(jnp.where(fwd, i, n_sub - 1 - i) * ch, ch)
        for h in range(GLA_HEADS):
            qc = q_ref[pl.ds(r0, ch), h * dk:(h + 1) * dk]
            kc = k_ref[pl.ds(r0, ch), h * dk:(h + 1) * dk]
            vc = v_ref[pl.ds(r0, ch), h * dv:(h + 1) * dv].astype(BF16)
            ac = la_ref[0, pl.ds(r0, ch), h * dk:(h + 1) * dk]
            cum = jnp.dot(tri_f, ac, precision=lax.Precision.HIGHEST, preferred_element_type=F32)
            total = jnp.sum(ac, axis=0, keepdims=True)
            qe = (qc * jnp.exp(cum)).astype(BF16)
            ke = (kc * jnp.exp(-cum)).astype(BF16)
            kd = (kc * jnp.exp(total - cum)).astype(BF16)
            scores = jnp.where(tri, _dot_nt(qe, ke), 0.0).astype(BF16)
            s_old = s_scr[h]
            o_ref[0, pl.ds(r0, ch), h * dv:(h + 1) * dv] = _dot(qe, s_old.astype(BF16)) + _dot(scores, vc)
            decay_t = jnp.transpose(jnp.broadcast_to(jnp.exp(total), (LANE, dk)))
            decay = jnp.concatenate([decay_t] * (dv // LANE), axis=1)
            s_scr[h] = decay * s_old + _dot_tn(kd, vc)
        return carry

    lax.fori_loop(0, n_sub, sub, 0)

    @pl.when(c == pl.num_programs(2) - 1)
    def _():
        sf_ref[0, 0] = s_scr[...]


def _gla_scan(p, la, s0, *, batch, seq, dk, dv, n_sub=4):
    h = GLA_HEADS
    n_sub = min(n_sub, seq // GLA_CHUNK)
    rb = n_sub * GLA_CHUNK
    nblk = seq // rb
    kq, kv = h * dk, h * dv
    blk = lambda b, d, c: b * nblk + c + d * (nblk - 1 - 2 * c)
    o, s_fin = pl.pallas_call(
        functools.partial(_gla_scan_kernel, n_sub=n_sub, dk=dk, dv=dv),
        grid=(batch, 2, nblk),
        in_specs=[
            pl.BlockSpec((rb, kq), lambda b, d, c: (blk(b, d, c), 0)),
            pl.BlockSpec((rb, kq), lambda b, d, c: (blk(b, d, c), 1)),
            pl.BlockSpec((rb, kv), lambda b, d, c: (blk(b, d, c), (2 * kq) // kv)),
            pl.BlockSpec((1, rb, kq), lambda b, d, c: (d, blk(b, d, c), 0)),
            pl.BlockSpec((1, 1, h, dk, dv), lambda b, d, c: (b, d, 0, 0, 0)),
        ],
        out_specs=[
            pl.BlockSpec((1, rb, kv), lambda b, d, c: (d, blk(b, d, c), 0)),
            pl.BlockSpec((1, 1, h, dk, dv), lambda b, d, c: (b, d, 0, 0, 0)),
        ],
        out_shape=[
            jax.ShapeDtypeStruct((2, batch * seq, kv), F32),
            jax.ShapeDtypeStruct((batch, 2, h, dk, dv), F32),
        ],
        scratch_shapes=[pltpu.VMEM((h, dk, dv), F32)],
        compiler_params=_cparams("parallel", "parallel", "arbitrary"),
    )(p, p, p, la, s0)
    return o, s_fin


def _na_kernel(q_ref, k_ref, v_ref, kc_ref, vc_ref, bias_ref, o_ref, *, rows, scale, plan):
    kc = kc_ref[...]
    vc = vc_ref[...]
    w = GRID_W
    n_q = NA_BLOCK_ROWS * w
    n_k = (NA_BLOCK_ROWS + NA_WIN_R) * w
    common, exceptions = plan

    def block(b, carry):
        r0 = b * NA_BLOCK_ROWS
        ws = jnp.clip(r0 - NA_WIN_R // 2, 0, rows - (NA_BLOCK_ROWS + NA_WIN_R))
        pid = common
        for blk, p in exceptions:
            pid = jnp.where(b == blk, p, pid)
        q0 = pl.multiple_of(r0 * w, w)
        k0 = pl.multiple_of(ws * w, w)
        q = q_ref[pl.ds(q0, n_q), :]
        kw = k_ref[pl.ds(k0, n_k), :]
        vw = v_ref[pl.ds(k0, n_k), :]
        s_lat = _dot_nt(q, kw) * scale + bias_ref[0, pid]
        s_ctx = _dot_nt(q, kc) * scale
        m = jnp.maximum(jnp.max(s_lat, axis=-1, keepdims=True), jnp.max(s_ctx, axis=-1, keepdims=True))
        p_lat = jnp.exp(s_lat - m)
        p_ctx = jnp.exp(s_ctx - m)
        denom = jnp.sum(p_lat, axis=-1, keepdims=True) + jnp.sum(p_ctx, axis=-1, keepdims=True)
        acc = _dot(p_lat.astype(BF16), vw) + _dot(p_ctx.astype(BF16), vc)
        o_ref[pl.ds(q0, n_q), :] = (acc / denom).astype(o_ref.dtype)
        return carry

    lax.fori_loop(0, rows // NA_BLOCK_ROWS, block, 0, unroll=NA_BLOCK_UNROLL)


def _na_plan(rows):
    kw_rows = NA_BLOCK_ROWS + NA_WIN_R
    patterns, ids = [], []
    for b in range(rows // NA_BLOCK_ROWS):
        r0 = b * NA_BLOCK_ROWS
        ws = min(max(r0 - NA_WIN_R // 2, 0), rows - kw_rows)
        key = []
        for rho in range(NA_BLOCK_ROWS):
            r = r0 + rho
            rs = min(max(r - NA_WIN_R // 2, 0), rows - NA_WIN_R)
            assert ws <= rs and rs + NA_WIN_R <= ws + kw_rows
            key.append((rs - ws, ws - r + NA_WIN_R - 1))
        key = tuple(key)
        if key not in patterns:
            patterns.append(key)
        ids.append(patterns.index(key))
    common = max(set(ids), key=ids.count)
    exceptions = tuple((b, p) for b, p in enumerate(ids) if p != common)
    dr = [[[min(max(base + om, 0), 2 * NA_WIN_R - 2) for om in range(kw_rows)] for (_, base) in pat]
          for pat in patterns]
    valid = [[[first <= om < first + NA_WIN_R for om in range(kw_rows)] for (first, _) in pat]
             for pat in patterns]
    return dr, valid, (common, exceptions)


def _na_bias_table(rpb, rows):
    w = GRID_W
    dr, valid_rows, plan = _na_plan(rows)
    col = jnp.arange(w)
    cs = jnp.clip(col - NA_WIN_C // 2, 0, w - NA_WIN_C)
    valid_cols = (col[None, :] >= cs[:, None]) & (col[None, :] < cs[:, None] + NA_WIN_C)
    rel = jnp.clip(col[None, :] - col[:, None] + (NA_WIN_C - 1), 0, 2 * NA_WIN_C - 2)
    cb = jnp.where(valid_cols[None, None], rpb[:, :, rel], MASK_BIAS)
    t = cb[:, jnp.asarray(dr)]
    t = jnp.where(jnp.asarray(valid_rows)[None, :, :, :, None, None], t, MASK_BIAS)
    n_pat = len(dr)
    t = t.transpose(0, 1, 2, 4, 3, 5).reshape(
        rpb.shape[0], n_pat, NA_BLOCK_ROWS * w, (NA_BLOCK_ROWS + NA_WIN_R) * w)
    return t.astype(F32), plan


def _na_attention(qkv, qkv_ctx, bias, plan, *, batch, seq, ctx_len, d):
    hd = d // NA_HEADS
    rows = seq // GRID_W
    assert rows >= NA_BLOCK_ROWS + NA_WIN_R and rows % NA_BLOCK_ROWS == 0 and hd == LANE
    nh = NA_HEADS
    return pl.pallas_call(
        functools.partial(_na_kernel, rows=rows, scale=hd ** -0.5, plan=plan),
        grid=(batch, nh),
        in_specs=[
            pl.BlockSpec((seq, hd), lambda b, h: (b, h)),
            pl.BlockSpec((seq, hd), lambda b, h: (b, nh + h)),
            pl.BlockSpec((seq, hd), lambda b, h: (b, 2 * nh + h)),
            pl.BlockSpec((ctx_len, hd), lambda b, h: (b, nh + h)),
            pl.BlockSpec((ctx_len, hd), lambda b, h: (b, 2 * nh + h)),
            pl.BlockSpec((1,) + bias.shape[1:], lambda b, h: (h, 0, 0, 0)),
        ],
        out_specs=pl.BlockSpec((seq, hd), lambda b, h: (b, h)),
        out_shape=jax.ShapeDtypeStruct((batch * seq, d), BF16),
        compiler_params=_cparams("parallel", "parallel"),
    )(qkv, qkv, qkv, qkv_ctx, qkv_ctx, bias)


def _finish(a, w_ref, b_ref, x_ref, mod_ref, g_ref, o_ref):
    y = _dot(a, w_ref[...])
    if b_ref is not None:
        y = y + b_ref[...]
    m = mod_ref[0]
    o_ref[...] = x_ref[...] + m[5:6, :] * _rms(y, g_ref[3:4, :])


def _out_na_kernel(a_ref, w_ref, x_ref, mod_ref, g_ref, o_ref):
    _finish(a_ref[...], w_ref, None, x_ref, mod_ref, g_ref, o_ref)


def _out_gla_kernel(of_ref, ob_ref, r_ref, ng_ref, w_ref, x_ref, mod_ref, g_ref, o_ref, a_scr, *, dv):
    ng = ng_ref[...]
    for h in range(GLA_HEADS):
        sl = slice(h * dv, (h + 1) * dv)
        o = of_ref[0, :, sl] + ob_ref[0, :, sl]
        a_scr[:, sl] = (_rms(o, ng) * r_ref[:, sl]).astype(BF16)
    _finish(a_scr[...], w_ref, None, x_ref, mod_ref, g_ref, o_ref)


def _out_conv_kernel(prev_ref, cur_ref, next_ref, wdw_ref, bdw_ref, lng_ref, lnb_ref, w_ref, b_ref,
                     x_ref, mod_ref, g_ref, o_ref, ext_scr, u_scr, *, seq, width, rb, ls):
    tm = cur_ref.shape[0]
    pos = (pl.program_id(0) * tm) % seq
    halo = CONV_HALO
    ext_scr[0:halo, :] = jnp.where(pos > 0, prev_ref[...], 0.0)
    ext_scr[halo:halo + tm, :] = cur_ref[...]
    ext_scr[halo + tm:, :] = jnp.where(pos + tm < seq, next_ref[...], 0.0)
    off = halo - width // 2
    d = cur_ref.shape[1]
    n_q = (off + width - 1) // SUBLANE + 1

    def rows(blk, carry):
        r0 = pl.multiple_of(blk * rb, rb)
        for c0 in range(0, d, ls):
            v = ext_scr[pl.ds(r0, rb + 2 * halo), c0:c0 + ls]
            acc = jnp.broadcast_to(bdw_ref[:, c0:c0 + ls], (rb, ls))
            for s in range(SUBLANE):
                vs = v[s:s + rb + SUBLANE * (n_q - 1)]
                for q in range(n_q):
                    k = SUBLANE * q + s - off
                    if 0 <= k < width:
                        wk = jnp.concatenate([wdw_ref[k, :, c0:c0 + ls]] * (rb // SUBLANE), axis=0)
                        acc = acc + wk * vs[SUBLANE * q:SUBLANE * q + rb]
            u_scr[pl.ds(r0, rb), c0:c0 + ls] = acc
        return carry

    lax.fori_loop(0, tm // rb, rows, 0)
    u = u_scr[...]
    mu = jnp.mean(u, axis=-1, keepdims=True)
    uc = u - mu
    var = jnp.mean(uc * uc, axis=-1, keepdims=True)
    y = uc * lax.rsqrt(var + NORM_EPS) * lng_ref[...] + lnb_ref[...]
    _finish(_silu(y).astype(BF16), w_ref, b_ref, x_ref, mod_ref, g_ref, o_ref)


def _out_tail_specs(tm, d, rows_per_mod, g):
    return [
        pl.BlockSpec((d, d), lambda i: (0, 0)),
        pl.BlockSpec((tm, d), lambda i: (i, 0)),
        pl.BlockSpec((1, N_MOD, d), lambda i: ((i * tm) // rows_per_mod, 0, 0)),
        pl.BlockSpec(g.shape, lambda i: (0, 0)),
    ]


def _out_call(kernel, head_specs, head_args, w, x, mod, rows_per_mod, g, tm, scratch=()):
    m_rows, d = x.shape
    tail = _out_tail_specs(tm, d, rows_per_mod, g)
    return pl.pallas_call(
        kernel,
        grid=(m_rows // tm,),
        in_specs=head_specs + tail,
        out_specs=pl.BlockSpec((tm, d), lambda i: (i, 0)),
        out_shape=jax.ShapeDtypeStruct((m_rows, d), F32),
        scratch_shapes=list(scratch),
        compiler_params=_cparams("parallel"),
    )(*head_args, w, x, mod, g)


def _out_na(a, w, x, mod, rows_per_mod, g, tm=512):
    d = x.shape[1]
    tm = min(tm, rows_per_mod)
    return _out_call(_out_na_kernel, [pl.BlockSpec((tm, d), lambda i: (i, 0))], [a],
                     w, x, mod, rows_per_mod, g, tm)


def _out_gla(o, p, norm_g, w, x, mod, rows_per_mod, g, *, dv, tm=256):
    d = x.shape[1]
    tm = min(tm, rows_per_mod)
    r_blk = (p.shape[1] - d) // d
    specs = [
        pl.BlockSpec((1, tm, d), lambda i: (0, i, 0)),
        pl.BlockSpec((1, tm, d), lambda i: (1, i, 0)),
        pl.BlockSpec((tm, d), lambda i: (i, r_blk)),
        pl.BlockSpec((1, dv), lambda i: (0, 0)),
    ]
    return _out_call(functools.partial(_out_gla_kernel, dv=dv), specs, [o, o, p, norm_g.reshape(1, dv)],
                     w, x, mod, rows_per_mod, g, tm, scratch=[pltpu.VMEM((tm, d), BF16)])


def _out_conv(u, w_dw, b_dw, ln_g, ln_b, w, b, x, mod, rows_per_mod, g, tm=256,
              rb=CONV_ROW_BLOCK, ls=CONV_LANE_BLOCK):
    m_rows, d = x.shape
    tm = min(tm, rows_per_mod)
    width = w_dw.shape[0]
    assert width // 2 < CONV_HALO and tm % CONV_HALO == 0 and tm % rb == 0 and d % ls == 0
    hb = tm // CONV_HALO
    last = m_rows // CONV_HALO - 1
    row = lambda v: v.reshape(1, d)
    specs = [
        pl.BlockSpec((CONV_HALO, d), lambda i: (jnp.maximum(i * hb - 1, 0), 0)),
        pl.BlockSpec((tm, d), lambda i: (i, 0)),
        pl.BlockSpec((CONV_HALO, d), lambda i: (jnp.minimum((i + 1) * hb, last), 0)),
        pl.BlockSpec((width, SUBLANE, d), lambda i: (0, 0, 0)),
    ] + [pl.BlockSpec((1, d), lambda i: (0, 0))] * 3
    kernel = functools.partial(_out_conv_kernel, seq=rows_per_mod, width=width, rb=rb, ls=ls)
    tail = _out_tail_specs(tm, d, rows_per_mod, g)
    return pl.pallas_call(
        kernel,
        grid=(m_rows // tm,),
        in_specs=specs + [tail[0], pl.BlockSpec((1, d), lambda i: (0, 0))] + tail[1:],
        out_specs=pl.BlockSpec((tm, d), lambda i: (i, 0)),
        out_shape=jax.ShapeDtypeStruct((m_rows, d), F32),
        scratch_shapes=[pltpu.VMEM((tm + 2 * CONV_HALO, d), F32), pltpu.VMEM((tm, d), F32)],
        compiler_params=_cparams("parallel"),
    )(u, u, u, jnp.broadcast_to(w_dw[:, None, :], (width, SUBLANE, d)),
      row(b_dw), row(ln_g), row(ln_b), w, row(b), x, mod, g)


def _rope_tables(seq, dk):
    n = dk // 2
    inv = ROPE_THETA ** (-jnp.arange(0, n, 2, dtype=F32) / n)
    t = jnp.arange(seq)

    def half(pos):
        ang = pos.astype(F32)[:, None] * inv
        c, s = jnp.cos(ang), jnp.sin(ang)
        return jnp.concatenate([c, c], axis=-1), jnp.concatenate([-s, s], axis=-1)

    cr, sr = half(t // GRID_W)
    cc, sc = half(t % GRID_W)
    return jnp.concatenate([cr, cc], axis=-1), jnp.concatenate([sr, sc], axis=-1)


def kernel(x, c, ctx, c_ctx, ada_w, ada_b, norm_g, ffn_w_gate, ffn_w_up, ffn_w_down, conv_w_in, conv_b_in, conv_w_dw, conv_b_dw, conv_ln_g, conv_ln_b, conv_w_out, conv_b_out, gla_w_q, gla_w_k, gla_w_v, gla_w_r, gla_b_r, gla_w_a1, gla_w_a2, gla_b_a, gla_norm_g, gla_w_o, na_w_qkv, na_rpb, na_w_o):
    batch, seq, d = x.shape
    ctx_len = ctx.shape[1]
    depth = ada_w.shape[0]
    dk_total = gla_w_q.shape[-1]
    dv_total = gla_w_v.shape[-1]
    dk, dv = dk_total // GLA_HEADS, dv_total // GLA_HEADS

    pad = (-(batch + 1)) % 8
    cs = jnp.concatenate([c, c_ctx[None, :], jnp.zeros((pad, d), F32)], axis=0)
    mods = _ada(cs, ada_w, ada_b).reshape(depth, batch + 1 + pad, N_MOD, d)

    wg_all, wu_all, wd_all = (w.astype(BF16) for w in (ffn_w_gate, ffn_w_up, ffn_w_down))

    x_lat = x.reshape(batch * seq, d)
    x_ctx = ctx.reshape(batch * ctx_len, d)
    reads_ctx = [(i % N_MIXERS) != 0 for i in range(depth)]

    for i in range(depth):
        kind, j = i % N_MIXERS, i // N_MIXERS
        ctx_out = any(reads_ctx[i + 1:])
        ctx_in = ctx_out or reads_ctx[i]
        mod_lat = mods[i, :batch]
        mod_ctx = jnp.broadcast_to(mods[i, batch], (batch, N_MOD, d))
        g = norm_g[i]
        lat = dict(mod=mod_lat, rows_per_mod=seq, g=g)
        cx = dict(mod=mod_ctx, rows_per_mod=ctx_len, g=g)

        ffn1 = dict(wg=wg_all, wu=wu_all, wd=wd_all, layer=i, which=0)
        ffn2 = dict(wg=wg_all, wu=wu_all, wd=wd_all, layer=i, which=1)

        cx_ffn = dict(mod=mod_ctx[:1], rows_per_mod=batch * ctx_len, g=g)
        x_lat = _ffn(x_lat, **lat, **ffn1)
        if ctx_in:
            x_ctx = _ffn(x_ctx, **cx_ffn, **ffn1)

        x_ctx_mixed = None
        if kind == 0:
            w_in = conv_w_in[j].astype(BF16)
            w_out = conv_w_out[j].astype(BF16)
            tail = (conv_w_dw[j], conv_b_dw[j], conv_ln_g[j], conv_ln_b[j], w_out, conv_b_out[j])
            u = _proj_glu(x_lat, w=w_in, b=conv_b_in[j], k0=3, gi=2, **lat)
            x_lat = _out_conv(u, *tail, x_lat, mod_lat, seq, g)
            if ctx_out:
                uc = _proj_glu(x_ctx, w=w_in, b=conv_b_in[j], k0=3, gi=2, **cx)
                x_ctx_mixed = _out_conv(uc, *tail, x_ctx, mod_ctx, ctx_len, g)
        elif kind == 1:
            w_cat = jnp.concatenate([gla_w_q[j], gla_w_k[j], gla_w_v[j], gla_w_r[j]], axis=1).astype(BF16)
            a1_cat = jnp.concatenate(
                [gla_w_a1[j, 0], gla_w_a1[j, 1], jnp.zeros((d, LANE - 2 * GLA_GATE_RANK), F32)], axis=1).astype(BF16)
            a2_pad = jnp.zeros((2, LANE, dk_total), F32)
            for dd in range(2):
                a2_pad = a2_pad.at[dd, dd * GLA_GATE_RANK:(dd + 1) * GLA_GATE_RANK].set(gla_w_a2[j, dd])
            a2_pad = a2_pad.astype(BF16)
            w_o = gla_w_o[j].astype(BF16)
            cos, sin = _rope_tables(seq, dk)
            proj = dict(w_cat=w_cat, b_r=gla_b_r[j], k0=3, gi=2, dk_total=dk_total, dv_total=dv_total)
            gate = dict(a1_cat=a1_cat, a2_pad=a2_pad, b_a=gla_b_a[j], k0=3, gi=2)

            p_ctx = _proj_gla(x_ctx, cos=cos, sin=sin, rope=False, **cx, **proj)
            la_ctx = _gla_gate(x_ctx, **cx, **gate)
            s0 = jnp.zeros((batch, 2, GLA_HEADS, dk, dv), F32)
            o_ctx, s_ctx = _gla_scan(p_ctx, la_ctx, s0, batch=batch, seq=ctx_len, dk=dk, dv=dv)

            p_lat = _proj_gla(x_lat, cos=cos, sin=sin, rope=True, **lat, **proj)
            la_lat = _gla_gate(x_lat, **lat, **gate)
            o_lat, _ = _gla_scan(p_lat, la_lat, s_ctx, batch=batch, seq=seq, dk=dk, dv=dv)

            x_lat = _out_gla(o_lat, p_lat, gla_norm_g[j], w_o, x_lat, mod_lat, seq, g, dv=dv)
            if ctx_out:
                x_ctx_mixed = _out_gla(o_ctx, p_ctx, gla_norm_g[j], w_o, x_ctx, mod_ctx, ctx_len, g, dv=dv)
        else:
            assert not ctx_out, "context self-attention output is not needed for this depth"
            w_qkv = na_w_qkv[j].astype(BF16)
            qkv = _proj_plain(x_lat, w=w_qkv, k0=3, gi=2, out_dtype=BF16, **lat)
            qkv_ctx = _proj_plain(x_ctx, w=w_qkv, k0=3, gi=2, out_dtype=BF16, **cx)
            bias, plan = _na_bias_table(na_rpb[j], seq // GRID_W)
            a = _na_attention(qkv, qkv_ctx, bias, plan, batch=batch, seq=seq, ctx_len=ctx_len, d=d)
            x_lat = _out_na(a, na_w_o[j].astype(BF16), x_lat, mod_lat, seq, g)

        x_lat = _ffn(x_lat, **lat, **ffn2)
        if ctx_out:
            x_ctx = _ffn(x_ctx_mixed, **cx_ffn, **ffn2)

    return x_lat.reshape(batch, seq, d)
```

```python
import functools
import math

import jax
import jax.numpy as jnp
from jax import lax
from jax.experimental import pallas as pl
from jax.experimental.pallas import tpu as pltpu

F32 = jnp.float32
BF16 = jnp.bfloat16

N_MIXERS = 3
N_MOD = 9
NORM_EPS = 1e-6
GRID_W = 64
GLA_HEADS = 4
GLA_GATE_RANK = 16
GLA_GATE_NORMALIZER = 16.0
GLA_CHUNK = 64
ROPE_THETA = 10000.0
NA_HEADS = 16
NA_WIN_R = 8
NA_WIN_C = 16
MASK_BIAS = -1e30
NA_BLOCK_ROWS = 4
NA_BLOCK_UNROLL = 4

LANE = 128
SUBLANE = 8
BF16_ROWS = 16
CONV_HALO = 16
CONV_ROW_BLOCK = 64
CONV_LANE_BLOCK = 128
GLU_COL_CHUNK = 512
VMEM_LIMIT = 56 * 1024 * 1024


def _cparams(*sem):
    return pltpu.CompilerParams(dimension_semantics=sem, vmem_limit_bytes=VMEM_LIMIT)


def _rms(x, g):
    return x * lax.rsqrt(jnp.mean(x * x, axis=-1, keepdims=True) + NORM_EPS) * g


def _silu(x):
    return x * jax.nn.sigmoid(x)


def _dot(a, b):
    return jnp.dot(a, b, preferred_element_type=F32)


def _dot_nt(a, b):
    return lax.dot_general(a, b, (((1,), (1,)), ((), ())), preferred_element_type=F32)


def _dot_tn(a, b):
    return lax.dot_general(a, b, (((0,), (0,)), ((), ())), preferred_element_type=F32)


def _modulated(x_ref, mod_ref, g_ref, k0, gi):
    m = mod_ref[0]
    y = _rms(x_ref[...], g_ref[gi:gi + 1, :])
    return (y * (1.0 + m[k0 + 1:k0 + 2, :]) + m[k0:k0 + 1, :]).astype(BF16)


def _ada_kernel(c_ref, w_ref, b_ref, o_ref):
    s = _silu(c_ref[...]).astype(BF16)
    o_ref[0] = _dot(s, w_ref[0].astype(BF16)) + b_ref[0]


def _ada(cs, ada_w, ada_b, tn=1024):
    depth, d, n = ada_w.shape
    r = cs.shape[0]
    return pl.pallas_call(
        _ada_kernel,
        grid=(depth, n // tn),
        in_specs=[
            pl.BlockSpec((r, d), lambda i, j: (0, 0)),
            pl.BlockSpec((1, d, tn), lambda i, j: (i, 0, j)),
            pl.BlockSpec((1, 1, tn), lambda i, j: (i, 0, j)),
        ],
        out_specs=pl.BlockSpec((1, r, tn), lambda i, j: (i, 0, j)),
        out_shape=jax.ShapeDtypeStruct((depth, r, n), F32),
        compiler_params=_cparams("parallel", "parallel"),
    )(cs, ada_w, ada_b.reshape(depth, 1, n))


def _with_tile_input(refs, k0, gi, chunk, body):
    x_ref, xn_ref, mod_ref, modn_ref, g_ref, h0_scr, h1_scr = refs
    i = pl.program_id(0)
    j = pl.program_id(1)
    tm = x_ref.shape[0]

    @pl.when(jnp.logical_and(i == 0, j == 0))
    def _():
        h0_scr[...] = _modulated(x_ref, mod_ref, g_ref, k0, gi)

    r0 = pl.multiple_of(jnp.minimum(j * chunk, tm - chunk), BF16_ROWS)

    def step(h_cur, h_next):
        h_next[pl.ds(r0, chunk), :] = _modulated(xn_ref.at[pl.ds(r0, chunk), :], modn_ref, g_ref, k0, gi)
        body(h_cur[...])

    @pl.when(i % 2 == 0)
    def _():
        step(h0_scr, h1_scr)

    @pl.when(i % 2 == 1)
    def _():
        step(h1_scr, h0_scr)


def _tile_input_specs(tm, d, rows_per_mod, g, n_tiles, n_steps, x_every_tile=True):
    chunk = -(-tm // (n_steps * BF16_ROWS)) * BF16_ROWS
    assert chunk <= tm and tm % BF16_ROWS == 0
    nxt = lambda i: jnp.minimum(i + 1, n_tiles - 1)
    x_spec = (pl.BlockSpec((tm, d), lambda i, j: (i, 0)) if x_every_tile else
              pl.BlockSpec((tm, d), lambda i, j: (0, 0), pipeline_mode=pl.Buffered(1)))
    specs = [
        x_spec,
        pl.BlockSpec((tm, d), lambda i, j: (nxt(i), 0)),
        pl.BlockSpec((1, N_MOD, d), lambda i, j: ((i * tm) // rows_per_mod, 0, 0)),
        pl.BlockSpec((1, N_MOD, d), lambda i, j: ((nxt(i) * tm) // rows_per_mod, 0, 0)),
        pl.BlockSpec(g.shape, lambda i, j: (0, 0)),
    ]
    return specs, [pltpu.VMEM((tm, d), BF16), pltpu.VMEM((tm, d), BF16)], chunk


def _ffn_kernel(x_ref, xn_ref, mod_ref, modn_ref, g_ref, wg_ref, wu_ref, wd_ref, o_ref,
                h0_scr, h1_scr, acc_scr, *, k0, gi, chunk):
    j = pl.program_id(1)

    @pl.when(j == 0)
    def _():
        acc_scr[...] = jnp.zeros_like(acc_scr)

    def body(h):
        gate = _dot(h, wg_ref[...])
        up = _dot(h, wu_ref[...])
        a = (_silu(gate) * up).astype(BF16)
        acc_scr[...] += _dot(a, wd_ref[...])

    _with_tile_input((x_ref, xn_ref, mod_ref, modn_ref, g_ref, h0_scr, h1_scr), k0, gi, chunk, body)

    @pl.when(j == pl.num_programs(1) - 1)
    def _():
        m = mod_ref[0]
        y = _rms(acc_scr[...], g_ref[gi + 1:gi + 2, :])
        o_ref[...] = x_ref[...] + (0.5 * m[k0 + 2:k0 + 3, :]) * y


def _ffn(x, mod, rows_per_mod, g, wg, wu, wd, *, layer, which, tm=512, tf=512):
    m_rows, d = x.shape
    f = wg.shape[-1]
    tm = min(tm, rows_per_mod)
    k0, gi = (0, 0) if which == 0 else (6, 4)
    n_tiles, n_steps = m_rows // tm, f // tf
    tile_specs, h_scratch, chunk = _tile_input_specs(tm, d, rows_per_mod, g, n_tiles, n_steps)
    return pl.pallas_call(
        functools.partial(_ffn_kernel, k0=k0, gi=gi, chunk=chunk),
        grid=(n_tiles, n_steps),
        in_specs=tile_specs + [
            pl.BlockSpec((None, None, d, tf), lambda i, j: (layer, which, 0, j)),
            pl.BlockSpec((None, None, d, tf), lambda i, j: (layer, which, 0, j)),
            pl.BlockSpec((None, None, tf, d), lambda i, j: (layer, which, j, 0)),
        ],
        out_specs=pl.BlockSpec((tm, d), lambda i, j: (i, 0)),
        out_shape=jax.ShapeDtypeStruct((m_rows, d), F32),
        scratch_shapes=h_scratch + [pltpu.VMEM((tm, d), F32)],
        compiler_params=_cparams("arbitrary", "arbitrary"),
    )(x, x, mod, mod, g, wg, wu, wd)


def _proj_plain_kernel(x_ref, xn_ref, mod_ref, modn_ref, g_ref, w_ref, o_ref, h0_scr, h1_scr,
                       *, k0, gi, chunk):
    def body(h):
        o_ref[...] = _dot(h, w_ref[...]).astype(o_ref.dtype)

    _with_tile_input((x_ref, xn_ref, mod_ref, modn_ref, g_ref, h0_scr, h1_scr), k0, gi, chunk, body)


def _proj_glu_kernel(x_ref, xn_ref, mod_ref, modn_ref, g_ref, wa_ref, wg_ref, ba_ref, bg_ref, o_ref,
                     h0_scr, h1_scr, *, k0, gi, chunk):
    def body(h):
        for c0 in range(0, o_ref.shape[1], GLU_COL_CHUNK):
            cols = slice(c0, c0 + GLU_COL_CHUNK)
            a = _dot(h, wa_ref[:, cols]) + ba_ref[:, cols]
            gate = _dot(h, wg_ref[:, cols]) + bg_ref[:, cols]
            o_ref[:, cols] = a * jax.nn.sigmoid(gate)

    _with_tile_input((x_ref, xn_ref, mod_ref, modn_ref, g_ref, h0_scr, h1_scr), k0, gi, chunk, body)


def _rope_groups(acc, cos_ref, sin_ref, scale, o_ref):
    n_groups = acc.shape[1] // LANE
    for gidx in range(n_groups):
        xg = acc[:, gidx * LANE:(gidx + 1) * LANE]
        t = (gidx % 2) * LANE
        rot = pltpu.roll(xg, LANE // 2, axis=1)
        o_ref[:, gidx * LANE:(gidx + 1) * LANE] = (
            xg * cos_ref[:, t:t + LANE] + rot * sin_ref[:, t:t + LANE]) * scale


def _proj_gla_kernel(x_ref, xn_ref, mod_ref, modn_ref, g_ref, w_ref, b_ref, cos_ref, sin_ref, o_ref,
                     h0_scr, h1_scr, *, k0, gi, chunk, q_scale, n_qk, n_v, rope):
    j = pl.program_id(1)

    def body(h):
        acc = _dot(h, w_ref[...])

        @pl.when(j < 2 * n_qk)
        def _():
            scale = jnp.where(j < n_qk, q_scale, 1.0).astype(F32)
            if rope:
                _rope_groups(acc, cos_ref, sin_ref, scale, o_ref)
            else:
                o_ref[...] = acc * scale

        @pl.when(jnp.logical_and(j >= 2 * n_qk, j < 2 * n_qk + n_v))
        def _():
            o_ref[...] = acc

        @pl.when(j >= 2 * n_qk + n_v)
        def _():
            o_ref[...] = _silu(acc + b_ref[...])

    _with_tile_input((x_ref, xn_ref, mod_ref, modn_ref, g_ref, h0_scr, h1_scr), k0, gi, chunk, body)


def _gla_gate_kernel(x_ref, mod_ref, g_ref, a1_ref, a2_ref, ba_ref, o_ref, *, k0, gi):
    h = _modulated(x_ref, mod_ref, g_ref, k0, gi)
    low = _dot(h, a1_ref[...]).astype(BF16)
    tm = x_ref.shape[0]
    row = lax.broadcasted_iota(jnp.int32, (tm, tm), 0)
    col = lax.broadcasted_iota(jnp.int32, (tm, tm), 1)
    same_chunk = row // GLA_CHUNK == col // GLA_CHUNK
    for d in range(2):
        z = _dot(low, a2_ref[d]) + ba_ref[d]
        log_sig = jnp.minimum(z, 0.0) - jnp.log1p(jnp.exp(-jnp.abs(z)))
        la = log_sig * (1.0 / GLA_GATE_NORMALIZER)
        ordered = (col <= row) if d == 0 else (col >= row)
        tri = jnp.where(jnp.logical_and(same_chunk, ordered), 1.0, 0.0).astype(BF16)
        la_hi = la.astype(BF16)
        rest = la - la_hi.astype(F32)
        la_mid = rest.astype(BF16)
        la_lo = (rest - la_mid.astype(F32)).astype(BF16)
        o_ref[d] = _dot(tri, la_hi) + _dot(tri, la_mid) + _dot(tri, la_lo)


def _proj_call(kernel_fn, extra_specs, extra_args, x, mod, rows_per_mod, g, n, tn, out_dtype, tm, **static):
    m_rows, d = x.shape
    tm = min(tm, rows_per_mod)
    n_tiles, n_steps = m_rows // tm, n // tn
    tile_specs, h_scratch, chunk = _tile_input_specs(tm, d, rows_per_mod, g, n_tiles, n_steps,
                                                     x_every_tile=False)
    return pl.pallas_call(
        functools.partial(kernel_fn, chunk=chunk, **static),
        grid=(n_tiles, n_steps),
        in_specs=tile_specs + extra_specs(tm),
        out_specs=pl.BlockSpec((tm, tn), lambda i, j: (i, j)),
        out_shape=jax.ShapeDtypeStruct((m_rows, n), out_dtype),
        scratch_shapes=h_scratch,
        compiler_params=_cparams("arbitrary", "arbitrary"),
    )(x, x, mod, mod, g, *extra_args)


def _proj_plain(x, mod, rows_per_mod, g, w, *, k0, gi, out_dtype, tm=1024, tn=512):
    d, n = w.shape
    specs = lambda tm: [pl.BlockSpec((d, tn), lambda i, j: (0, j))]
    return _proj_call(_proj_plain_kernel, specs, (w,), x, mod, rows_per_mod, g, n, tn, out_dtype, tm,
                      k0=k0, gi=gi)


def _proj_glu(x, mod, rows_per_mod, g, w, b, *, k0, gi, tm=512, tn=2048):
    d = w.shape[0]
    n = w.shape[1] // 2
    nb = n // tn
    assert tn % GLU_COL_CHUNK == 0
    b2 = b.reshape(1, 2 * n)
    mode = dict(pipeline_mode=pl.Buffered(1)) if nb == 1 else {}
    specs = lambda tm: [
        pl.BlockSpec((d, tn), lambda i, j: (0, j), **mode),
        pl.BlockSpec((d, tn), lambda i, j: (0, j + nb), **mode),
        pl.BlockSpec((1, tn), lambda i, j: (0, j), **mode),
        pl.BlockSpec((1, tn), lambda i, j: (0, j + nb), **mode),
    ]
    return _proj_call(_proj_glu_kernel, specs, (w, w, b2, b2), x, mod, rows_per_mod, g, n, tn, F32, tm,
                      k0=k0, gi=gi)


def _proj_gla(x, mod, rows_per_mod, g, w_cat, b_r, cos, sin, *, k0, gi, dk_total, dv_total, rope,
              tm=1024, tn=512):
    d, n = w_cat.shape
    n_qk, n_v = dk_total // tn, dv_total // tn
    dk = dk_total // GLA_HEADS

    def specs(tm):
        seq_blocks = cos.shape[0] // tm
        tab_spec = pl.BlockSpec((tm, cos.shape[1]), lambda i, j: (i % seq_blocks, 0))
        return [
            pl.BlockSpec((d, tn), lambda i, j: (0, j)),
            pl.BlockSpec((1, tn), lambda i, j: (0, jnp.maximum(j - (2 * n_qk + n_v), 0))),
            tab_spec, tab_spec,
        ]

    return _proj_call(_proj_gla_kernel, specs, (w_cat, b_r.reshape(1, -1), cos, sin), x, mod, rows_per_mod, g,
                      n, tn, F32, tm, k0=k0, gi=gi, q_scale=dk ** -0.5, n_qk=n_qk, n_v=n_v, rope=rope)


def _gla_gate(x, mod, rows_per_mod, g, a1_cat, a2_pad, b_a, *, k0, gi, tm=256):
    m_rows, d = x.shape
    n = a2_pad.shape[2]
    tm = min(tm, rows_per_mod)
    return pl.pallas_call(
        functools.partial(_gla_gate_kernel, k0=k0, gi=gi),
        grid=(m_rows // tm,),
        in_specs=[
            pl.BlockSpec((tm, d), lambda i: (i, 0)),
            pl.BlockSpec((1, N_MOD, d), lambda i: ((i * tm) // rows_per_mod, 0, 0)),
            pl.BlockSpec(g.shape, lambda i: (0, 0)),
            pl.BlockSpec(a1_cat.shape, lambda i: (0, 0)),
            pl.BlockSpec(a2_pad.shape, lambda i: (0, 0, 0)),
            pl.BlockSpec((2, 1, n), lambda i: (0, 0, 0)),
        ],
        out_specs=pl.BlockSpec((2, tm, n), lambda i: (0, i, 0)),
        out_shape=jax.ShapeDtypeStruct((2, m_rows, n), F32),
        compiler_params=_cparams("parallel"),
    )(x, mod, g, a1_cat, a2_pad, b_a.reshape(2, 1, n))


def _gla_scan_kernel(q_ref, k_ref, v_ref, cum_ref, s0_ref, o_ref, sf_ref, s_scr, *, n_sub, dk, dv):
    d = pl.program_id(1)
    c = pl.program_id(2)
    fwd = d == 0

    @pl.when(c == 0)
    def _():
        s_scr[...] = s0_ref[0, 0]

    ch = GLA_CHUNK
    row = lax.broadcasted_iota(jnp.int32, (ch, ch), 0)
    col = lax.broadcasted_iota(jnp.int32, (ch, ch), 1)
    tri = (row - col) * (1 - 2 * d) >= 0

    def sub(i, carry):
        r0 = pl.multiple_of(jnp.where(fwd, i, n_sub - 1 - i) * ch, ch)
        for h in range(GLA_HEADS):
            qc = q_ref[pl.ds(r0, ch), h * dk:(h + 1) * dk]
            kc = k_ref[pl.ds(r0, ch), h * dk:(h + 1) * dk]
            vc = v_ref[pl.ds(r0, ch), h * dv:(h + 1) * dv].astype(BF16)
            cum = cum_ref[0, pl.ds(r0, ch), h * dk:(h + 1) * dk]
            total = jnp.where(fwd, cum[ch - 1:ch, :], cum[0:1, :])
            qe = (qc * jnp.exp(cum)).astype(BF16)
            ke = (kc * jnp.exp(-cum)).astype(BF16)
            kd = (kc * jnp.exp(total - cum)).astype(BF16)
            scores = jnp.where(tri, _dot_nt(qe, ke), 0.0).astype(BF16)
            s_old = s_scr[h]
            o_ref[0, pl.ds(r0, ch), h * dv:(h + 1) * dv] = _dot(qe, s_old.astype(BF16)) + _dot(scores, vc)
            decay_t = jnp.transpose(jnp.broadcast_to(jnp.exp(total), (LANE, dk)))
            decay = jnp.concatenate([decay_t] * (dv // LANE), axis=1)
            s_scr[h] = decay * s_old + _dot_tn(kd, vc)
        return carry

    lax.fori_loop(0, n_sub, sub, 0)

    @pl.when(c == pl.num_programs(2) - 1)
    def _():
        sf_ref[0, 0] = s_scr[...]


def _gla_scan(p, la, s0, *, batch, seq, dk, dv, n_sub=8):
    h = GLA_HEADS
    n_sub = min(n_sub, seq // GLA_CHUNK)
    rb = n_sub * GLA_CHUNK
    nblk = seq // rb
    kq, kv = h * dk, h * dv
    blk = lambda b, d, c: b * nblk + c + d * (nblk - 1 - 2 * c)
    o, s_fin = pl.pallas_call(
        functools.partial(_gla_scan_kernel, n_sub=n_sub, dk=dk, dv=dv),
        grid=(batch, 2, nblk),
        in_specs=[
            pl.BlockSpec((rb, kq), lambda b, d, c: (blk(b, d, c), 0)),
            pl.BlockSpec((rb, kq), lambda b, d, c: (blk(b, d, c), 1)),
            pl.BlockSpec((rb, kv), lambda b, d, c: (blk(b, d, c), (2 * kq) // kv)),
            pl.BlockSpec((1, rb, kq), lambda b, d, c: (d, blk(b, d, c), 0)),
            pl.BlockSpec((1, 1, h, dk, dv), lambda b, d, c: (b, d, 0, 0, 0)),
        ],
        out_specs=[
            pl.BlockSpec((1, rb, kv), lambda b, d, c: (d, blk(b, d, c), 0)),
            pl.BlockSpec((1, 1, h, dk, dv), lambda b, d, c: (b, d, 0, 0, 0)),
        ],
        out_shape=[
            jax.ShapeDtypeStruct((2, batch * seq, kv), F32),
            jax.ShapeDtypeStruct((batch, 2, h, dk, dv), F32),
        ],
        scratch_shapes=[pltpu.VMEM((h, dk, dv), F32)],
        compiler_params=_cparams("parallel", "parallel", "arbitrary"),
    )(p, p, p, la, s0)
    return o, s_fin


def _na_kernel(q_ref, k_ref, v_ref, kc_ref, vc_ref, bias_ref, o_ref, *, rows, scale, plan):
    kc = kc_ref[...]
    vc = vc_ref[...]
    w = GRID_W
    n_q = NA_BLOCK_ROWS * w
    n_k = (NA_BLOCK_ROWS + NA_WIN_R) * w
    common, exceptions = plan

    def block(b, carry):
        r0 = b * NA_BLOCK_ROWS
        ws = jnp.clip(r0 - NA_WIN_R // 2, 0, rows - (NA_BLOCK_ROWS + NA_WIN_R))
        pid = common
        for blk, p in exceptions:
            pid = jnp.where(b == blk, p, pid)
        q0 = pl.multiple_of(r0 * w, w)
        k0 = pl.multiple_of(ws * w, w)
        q = q_ref[pl.ds(q0, n_q), :]
        kw = k_ref[pl.ds(k0, n_k), :]
        vw = v_ref[pl.ds(k0, n_k), :]
        s_lat = _dot_nt(q, kw) * scale + bias_ref[0, pid]
        s_ctx = _dot_nt(q, kc) * scale
        m = jnp.maximum(jnp.max(s_lat, axis=-1, keepdims=True), jnp.max(s_ctx, axis=-1, keepdims=True))
        p_lat = jnp.exp(s_lat - m)
        p_ctx = jnp.exp(s_ctx - m)
        denom = jnp.sum(p_lat, axis=-1, keepdims=True) + jnp.sum(p_ctx, axis=-1, keepdims=True)
        acc = _dot(p_lat.astype(BF16), vw) + _dot(p_ctx.astype(BF16), vc)
        o_ref[pl.ds(q0, n_q), :] = (acc / denom).astype(o_ref.dtype)
        return carry

    lax.fori_loop(0, rows // NA_BLOCK_ROWS, block, 0, unroll=NA_BLOCK_UNROLL)


def _na_plan(rows):
    kw_rows = NA_BLOCK_ROWS + NA_WIN_R
    patterns, ids = [], []
    for b in range(rows // NA_BLOCK_ROWS):
        r0 = b * NA_BLOCK_ROWS
        ws = min(max(r0 - NA_WIN_R // 2, 0), rows - kw_rows)
        key = []
        for rho in range(NA_BLOCK_ROWS):
            r = r0 + rho
            rs = min(max(r - NA_WIN_R // 2, 0), rows - NA_WIN_R)
            assert ws <= rs and rs + NA_WIN_R <= ws + kw_rows
            key.append((rs - ws, ws - r + NA_WIN_R - 1))
        key = tuple(key)
        if key not in patterns:
            patterns.append(key)
        ids.append(patterns.index(key))
    common = max(set(ids), key=ids.count)
    exceptions = tuple((b, p) for b, p in enumerate(ids) if p != common)
    dr = [[[min(max(base + om, 0), 2 * NA_WIN_R - 2) for om in range(kw_rows)] for (_, base) in pat]
          for pat in patterns]
    valid = [[[first <= om < first + NA_WIN_R for om in range(kw_rows)] for (first, _) in pat]
             for pat in patterns]
    return dr, valid, (common, exceptions)


def _na_bias_table(rpb, rows):
    w = GRID_W
    dr, valid_rows, plan = _na_plan(rows)
    col = jnp.arange(w)
    cs = jnp.clip(col - NA_WIN_C // 2, 0, w - NA_WIN_C)
    valid_cols = (col[None, :] >= cs[:, None]) & (col[None, :] < cs[:, None] + NA_WIN_C)
    rel = jnp.clip(col[None, :] - col[:, None] + (NA_WIN_C - 1), 0, 2 * NA_WIN_C - 2)
    cb = jnp.where(valid_cols[None, None], rpb[:, :, rel], MASK_BIAS).astype(F32)
    masked = jnp.full_like(cb[:, 0], MASK_BIAS)
    strips = [jnp.concatenate([cb[:, k] if ok else masked for k, ok in zip(dr_row, ok_row)], axis=-1)
              for dr_pat, ok_pat in zip(dr, valid_rows) for dr_row, ok_row in zip(dr_pat, ok_pat)]
    t = jnp.stack(strips, axis=1)
    t = t.reshape(rpb.shape[0], len(dr), NA_BLOCK_ROWS * w, (NA_BLOCK_ROWS + NA_WIN_R) * w)
    return t, plan


def _na_attention(qkv, qkv_ctx, bias, plan, *, batch, seq, ctx_len, d):
    hd = d // NA_HEADS
    rows = seq // GRID_W
    assert rows >= NA_BLOCK_ROWS + NA_WIN_R and rows % NA_BLOCK_ROWS == 0 and hd == LANE
    nh = NA_HEADS
    return pl.pallas_call(
        functools.partial(_na_kernel, rows=rows, scale=hd ** -0.5, plan=plan),
        grid=(batch, nh),
        in_specs=[
            pl.BlockSpec((seq, hd), lambda b, h: (b, h)),
            pl.BlockSpec((seq, hd), lambda b, h: (b, nh + h)),
            pl.BlockSpec((seq, hd), lambda b, h: (b, 2 * nh + h)),
            pl.BlockSpec((ctx_len, hd), lambda b, h: (b, nh + h)),
            pl.BlockSpec((ctx_len, hd), lambda b, h: (b, 2 * nh + h)),
            pl.BlockSpec((1,) + bias.shape[1:], lambda b, h: (h, 0, 0, 0)),
        ],
        out_specs=pl.BlockSpec((seq, hd), lambda b, h: (b, h)),
        out_shape=jax.ShapeDtypeStruct((batch * seq, d), BF16),
        compiler_params=_cparams("parallel", "parallel"),
    )(qkv, qkv, qkv, qkv_ctx, qkv_ctx, bias)


def _finish(a, w_ref, b_ref, x_ref, mod_ref, g_ref, o_ref):
    y = _dot(a, w_ref[...])
    if b_ref is not None:
        y = y + b_ref[...]
    m = mod_ref[0]
    o_ref[...] = x_ref[...] + m[5:6, :] * _rms(y, g_ref[3:4, :])


def _out_na_kernel(a_ref, w_ref, x_ref, mod_ref, g_ref, o_ref):
    _finish(a_ref[...], w_ref, None, x_ref, mod_ref, g_ref, o_ref)


def _out_gla_kernel(of_ref, ob_ref, r_ref, ng_ref, w_ref, x_ref, mod_ref, g_ref, o_ref, a_scr, *, dv):
    ng = ng_ref[...]
    for h in range(GLA_HEADS):
        sl = slice(h * dv, (h + 1) * dv)
        o = of_ref[0, :, sl] + ob_ref[0, :, sl]
        a_scr[:, sl] = (_rms(o, ng) * r_ref[:, sl]).astype(BF16)
    _finish(a_scr[...], w_ref, None, x_ref, mod_ref, g_ref, o_ref)


def _out_conv_kernel(prev_ref, cur_ref, next_ref, wdw_ref, bdw_ref, lng_ref, lnb_ref, w_ref, b_ref,
                     x_ref, mod_ref, g_ref, o_ref, ext_scr, u_scr, *, seq, width, rb, ls):
    tm = cur_ref.shape[0]
    pos = (pl.program_id(0) * tm) % seq
    halo = CONV_HALO
    ext_scr[0:halo, :] = jnp.where(pos > 0, prev_ref[...], 0.0)
    ext_scr[halo:halo + tm, :] = cur_ref[...]
    ext_scr[halo + tm:, :] = jnp.where(pos + tm < seq, next_ref[...], 0.0)
    off = halo - width // 2
    d = cur_ref.shape[1]
    n_q = (off + width - 1) // SUBLANE + 1

    def rows(blk, carry):
        r0 = pl.multiple_of(blk * rb, rb)
        for c0 in range(0, d, ls):
            v = ext_scr[pl.ds(r0, rb + 2 * halo), c0:c0 + ls]
            acc = jnp.broadcast_to(bdw_ref[:, c0:c0 + ls], (rb, ls))
            for s in range(SUBLANE):
                vs = v[s:s + rb + SUBLANE * (n_q - 1)]
                for q in range(n_q):
                    k = SUBLANE * q + s - off
                    if 0 <= k < width:
                        wk = jnp.concatenate([wdw_ref[k, :, c0:c0 + ls]] * (rb // SUBLANE), axis=0)
                        acc = acc + wk * vs[SUBLANE * q:SUBLANE * q + rb]
            u_scr[pl.ds(r0, rb), c0:c0 + ls] = acc
        return carry

    lax.fori_loop(0, tm // rb, rows, 0)
    u = u_scr[...]
    mu = jnp.mean(u, axis=-1, keepdims=True)
    uc = u - mu
    var = jnp.mean(uc * uc, axis=-1, keepdims=True)
    y = uc * lax.rsqrt(var + NORM_EPS) * lng_ref[...] + lnb_ref[...]
    _finish(_silu(y).astype(BF16), w_ref, b_ref, x_ref, mod_ref, g_ref, o_ref)


def _out_tail_specs(tm, d, rows_per_mod, g):
    return [
        pl.BlockSpec((d, d), lambda i: (0, 0)),
        pl.BlockSpec((tm, d), lambda i: (i, 0)),
        pl.BlockSpec((1, N_MOD, d), lambda i: ((i * tm) // rows_per_mod, 0, 0)),
        pl.BlockSpec(g.shape, lambda i: (0, 0)),
    ]


def _out_call(kernel, head_specs, head_args, w, x, mod, rows_per_mod, g, tm, scratch=()):
    m_rows, d = x.shape
    tail = _out_tail_specs(tm, d, rows_per_mod, g)
    return pl.pallas_call(
        kernel,
        grid=(m_rows // tm,),
        in_specs=head_specs + tail,
        out_specs=pl.BlockSpec((tm, d), lambda i: (i, 0)),
        out_shape=jax.ShapeDtypeStruct((m_rows, d), F32),
        scratch_shapes=list(scratch),
        compiler_params=_cparams("parallel"),
    )(*head_args, w, x, mod, g)


def _out_na(a, w, x, mod, rows_per_mod, g, tm=512):
    d = x.shape[1]
    tm = min(tm, rows_per_mod)
    return _out_call(_out_na_kernel, [pl.BlockSpec((tm, d), lambda i: (i, 0))], [a],
                     w, x, mod, rows_per_mod, g, tm)


def _out_gla(o, p, norm_g, w, x, mod, rows_per_mod, g, *, dv, tm=256):
    d = x.shape[1]
    tm = min(tm, rows_per_mod)
    r_blk = (p.shape[1] - d) // d
    specs = [
        pl.BlockSpec((1, tm, d), lambda i: (0, i, 0)),
        pl.BlockSpec((1, tm, d), lambda i: (1, i, 0)),
        pl.BlockSpec((tm, d), lambda i: (i, r_blk)),
        pl.BlockSpec((1, dv), lambda i: (0, 0)),
    ]
    return _out_call(functools.partial(_out_gla_kernel, dv=dv), specs, [o, o, p, norm_g.reshape(1, dv)],
                     w, x, mod, rows_per_mod, g, tm, scratch=[pltpu.VMEM((tm, d), BF16)])


def _out_conv(u, w_dw, b_dw, ln_g, ln_b, w, b, x, mod, rows_per_mod, g, tm=256,
              rb=CONV_ROW_BLOCK, ls=CONV_LANE_BLOCK):
    m_rows, d = x.shape
    tm = min(tm, rows_per_mod)
    width = w_dw.shape[0]
    assert width // 2 < CONV_HALO and tm % CONV_HALO == 0 and tm % rb == 0 and d % ls == 0
    hb = tm // CONV_HALO
    last = m_rows // CONV_HALO - 1
    row = lambda v: v.reshape(1, d)
    specs = [
        pl.BlockSpec((CONV_HALO, d), lambda i: (jnp.maximum(i * hb - 1, 0), 0)),
        pl.BlockSpec((tm, d), lambda i: (i, 0)),
        pl.BlockSpec((CONV_HALO, d), lambda i: (jnp.minimum((i + 1) * hb, last), 0)),
        pl.BlockSpec((width, SUBLANE, d), lambda i: (0, 0, 0)),
    ] + [pl.BlockSpec((1, d), lambda i: (0, 0))] * 3
    kernel = functools.partial(_out_conv_kernel, seq=rows_per_mod, width=width, rb=rb, ls=ls)
    tail = _out_tail_specs(tm, d, rows_per_mod, g)
    return pl.pallas_call(
        kernel,
        grid=(m_rows // tm,),
        in_specs=specs + [tail[0], pl.BlockSpec((1, d), lambda i: (0, 0))] + tail[1:],
        out_specs=pl.BlockSpec((tm, d), lambda i: (i, 0)),
        out_shape=jax.ShapeDtypeStruct((m_rows, d), F32),
        scratch_shapes=[pltpu.VMEM((tm + 2 * CONV_HALO, d), F32), pltpu.VMEM((tm, d), F32)],
        compiler_params=_cparams("parallel"),
    )(u, u, u, jnp.broadcast_to(w_dw[:, None, :], (width, SUBLANE, d)),
      row(b_dw), row(ln_g), row(ln_b), w, row(b), x, mod, g)


def _rope_tables(seq, dk):
    n = dk // 2
    inv = ROPE_THETA ** (-jnp.arange(0, n, 2, dtype=F32) / n)
    t = jnp.arange(seq)

    def half(pos):
        ang = pos.astype(F32)[:, None] * inv
        c, s = jnp.cos(ang), jnp.sin(ang)
        return jnp.concatenate([c, c], axis=-1), jnp.concatenate([-s, s], axis=-1)

    cr, sr = half(t // GRID_W)
    cc, sc = half(t % GRID_W)
    return jnp.concatenate([cr, cc], axis=-1), jnp.concatenate([sr, sc], axis=-1)


def kernel(x, c, ctx, c_ctx, ada_w, ada_b, norm_g, ffn_w_gate, ffn_w_up, ffn_w_down, conv_w_in, conv_b_in, conv_w_dw, conv_b_dw, conv_ln_g, conv_ln_b, conv_w_out, conv_b_out, gla_w_q, gla_w_k, gla_w_v, gla_w_r, gla_b_r, gla_w_a1, gla_w_a2, gla_b_a, gla_norm_g, gla_w_o, na_w_qkv, na_rpb, na_w_o):
    batch, seq, d = x.shape
    ctx_len = ctx.shape[1]
    depth = ada_w.shape[0]
    dk_total = gla_w_q.shape[-1]
    dv_total = gla_w_v.shape[-1]
    dk, dv = dk_total // GLA_HEADS, dv_total // GLA_HEADS

    pad = (-(batch + 1)) % 8
    cs = jnp.concatenate([c, c_ctx[None, :], jnp.zeros((pad, d), F32)], axis=0)
    mods = _ada(cs, ada_w, ada_b).reshape(depth, batch + 1 + pad, N_MOD, d)

    wg_all, wu_all, wd_all = (w.astype(BF16) for w in (ffn_w_gate, ffn_w_up, ffn_w_down))

    x_lat = x.reshape(batch * seq, d)
    x_ctx = ctx.reshape(batch * ctx_len, d)
    reads_ctx = [(i % N_MIXERS) != 0 for i in range(depth)]

    for i in range(depth):
        kind, j = i % N_MIXERS, i // N_MIXERS
        ctx_out = any(reads_ctx[i + 1:])
        ctx_in = ctx_out or reads_ctx[i]
        mod_lat = mods[i, :batch]
        mod_ctx = jnp.broadcast_to(mods[i, batch], (batch, N_MOD, d))
        g = norm_g[i]
        lat = dict(mod=mod_lat, rows_per_mod=seq, g=g)
        cx = dict(mod=mod_ctx, rows_per_mod=ctx_len, g=g)

        ffn1 = dict(wg=wg_all, wu=wu_all, wd=wd_all, layer=i, which=0)
        ffn2 = dict(wg=wg_all, wu=wu_all, wd=wd_all, layer=i, which=1)

        cx_ffn = dict(mod=mod_ctx[:1], rows_per_mod=batch * ctx_len, g=g)
        x_lat = _ffn(x_lat, **lat, **ffn1)
        if ctx_in:
            x_ctx = _ffn(x_ctx, **cx_ffn, **ffn1)

        x_ctx_mixed = None
        if kind == 0:
            w_in = conv_w_in[j].astype(BF16)
            w_out = conv_w_out[j].astype(BF16)
            tail = (conv_w_dw[j], conv_b_dw[j], conv_ln_g[j], conv_ln_b[j], w_out, conv_b_out[j])
            u = _proj_glu(x_lat, w=w_in, b=conv_b_in[j], k0=3, gi=2, **lat)
            x_lat = _out_conv(u, *tail, x_lat, mod_lat, seq, g)
            if ctx_out:
                uc = _proj_glu(x_ctx, w=w_in, b=conv_b_in[j], k0=3, gi=2, **cx)
                x_ctx_mixed = _out_conv(uc, *tail, x_ctx, mod_ctx, ctx_len, g)
        elif kind == 1:
            w_cat = jnp.concatenate([gla_w_q[j], gla_w_k[j], gla_w_v[j], gla_w_r[j]], axis=1).astype(BF16)
            a1_cat = jnp.concatenate(
                [gla_w_a1[j, 0], gla_w_a1[j, 1], jnp.zeros((d, LANE - 2 * GLA_GATE_RANK), F32)], axis=1).astype(BF16)
            a2_pad = jnp.zeros((2, LANE, dk_total), F32)
            for dd in range(2):
                a2_pad = a2_pad.at[dd, dd * GLA_GATE_RANK:(dd + 1) * GLA_GATE_RANK].set(gla_w_a2[j, dd])
            a2_pad = a2_pad.astype(BF16)
            w_o = gla_w_o[j].astype(BF16)
            cos, sin = _rope_tables(seq, dk)
            proj = dict(w_cat=w_cat, b_r=gla_b_r[j], k0=3, gi=2, dk_total=dk_total, dv_total=dv_total)
            gate = dict(a1_cat=a1_cat, a2_pad=a2_pad, b_a=gla_b_a[j], k0=3, gi=2)

            p_ctx = _proj_gla(x_ctx, cos=cos, sin=sin, rope=False, **cx, **proj)
            la_ctx = _gla_gate(x_ctx, **cx, **gate)
            s0 = jnp.zeros((batch, 2, GLA_HEADS, dk, dv), F32)
            o_ctx, s_ctx = _gla_scan(p_ctx, la_ctx, s0, batch=batch, seq=ctx_len, dk=dk, dv=dv)

            p_lat = _proj_gla(x_lat, cos=cos, sin=sin, rope=True, **lat, **proj)
            la_lat = _gla_gate(x_lat, **lat, **gate)
            o_lat, _ = _gla_scan(p_lat, la_lat, s_ctx, batch=batch, seq=seq, dk=dk, dv=dv)

            x_lat = _out_gla(o_lat, p_lat, gla_norm_g[j], w_o, x_lat, mod_lat, seq, g, dv=dv)
            if ctx_out:
                x_ctx_mixed = _out_gla(o_ctx, p_ctx, gla_norm_g[j], w_o, x_ctx, mod_ctx, ctx_len, g, dv=dv)
        else:
            assert not ctx_out, "context self-attention output is not needed for this depth"
            w_qkv = na_w_qkv[j].astype(BF16)
            qkv = _proj_plain(x_lat, w=w_qkv, k0=3, gi=2, out_dtype=BF16, **lat)
            qkv_ctx = _proj_plain(x_ctx, w=w_qkv, k0=3, gi=2, out_dtype=BF16, **cx)
            bias, plan = _na_bias_table(na_rpb[j], seq // GRID_W)
            a = _na_attention(qkv, qkv_ctx, bias, plan, batch=batch, seq=seq, ctx_len=ctx_len, d=d)
            x_lat = _out_na(a, na_w_o[j].astype(BF16), x_lat, mod_lat, seq, g)

        x_lat = _ffn(x_lat, **lat, **ffn2)
        if ctx_out:
            x_ctx = _ffn(x_ctx_mixed, **cx_ffn, **ffn2)

    return x_lat.reshape(batch, seq, d)
```

```python
import functools

import jax
import jax.numpy as jnp
from jax import lax
from jax.experimental import pallas as pl
from jax.experimental.pallas import tpu as pltpu

F32 = jnp.float32
BF16 = jnp.bfloat16

N_MIXERS = 3
N_MOD = 9
NORM_EPS = 1e-6
GRID_W = 64
GLA_HEADS = 4
GLA_GATE_RANK = 16
GLA_GATE_NORMALIZER = 16.0
GLA_CHUNK = 64
ROPE_THETA = 10000.0
NA_HEADS = 16
NA_WIN_R = 8
NA_WIN_C = 16
MASK_BIAS = -1e30
NA_BLOCK_ROWS = 4
NA_BLOCK_UNROLL = 4

LANE = 128
SUBLANE = 8
BF16_ROWS = 16
CONV_HALO = 16
CONV_ROW_BLOCK = 64
CONV_LANE_BLOCK = 128
GLU_COL_CHUNK = 512
VMEM_LIMIT = 56 * 1024 * 1024


def _cparams(*sem):
    return pltpu.CompilerParams(dimension_semantics=sem, vmem_limit_bytes=VMEM_LIMIT)


def _rms(x, g):
    return x * lax.rsqrt(jnp.mean(x * x, axis=-1, keepdims=True) + NORM_EPS) * g


def _silu(x):
    return x * jax.nn.sigmoid(x)


def _dot(a, b):
    return jnp.dot(a, b, preferred_element_type=F32)


def _dot_nt(a, b):
    return lax.dot_general(a, b, (((1,), (1,)), ((), ())), preferred_element_type=F32)


def _dot_tn(a, b):
    return lax.dot_general(a, b, (((0,), (0,)), ((), ())), preferred_element_type=F32)


def _modulated(x_ref, mod_ref, g_ref, k0, gi):
    m = mod_ref[0]
    y = _rms(x_ref[...], g_ref[gi:gi + 1, :])
    return (y * (1.0 + m[k0 + 1:k0 + 2, :]) + m[k0:k0 + 1, :]).astype(BF16)


def _ada_kernel(c_ref, w_ref, b_ref, o_ref):
    s = _silu(c_ref[...]).astype(BF16)
    o_ref[0] = _dot(s, w_ref[0].astype(BF16)) + b_ref[0]


def _ada(cs, ada_w, ada_b, tn=1024):
    depth, d, n = ada_w.shape
    r = cs.shape[0]
    return pl.pallas_call(
        _ada_kernel,
        grid=(depth, n // tn),
        in_specs=[
            pl.BlockSpec((r, d), lambda i, j: (0, 0)),
            pl.BlockSpec((1, d, tn), lambda i, j: (i, 0, j)),
            pl.BlockSpec((1, 1, tn), lambda i, j: (i, 0, j)),
        ],
        out_specs=pl.BlockSpec((1, r, tn), lambda i, j: (i, 0, j)),
        out_shape=jax.ShapeDtypeStruct((depth, r, n), F32),
        compiler_params=_cparams("parallel", "parallel"),
    )(cs, ada_w, ada_b.reshape(depth, 1, n))


def _with_tile_input(refs, k0, gi, chunk, body):
    x_ref, xn_ref, mod_ref, modn_ref, g_ref, h0_scr, h1_scr = refs
    i = pl.program_id(0)
    j = pl.program_id(1)
    tm = x_ref.shape[0]

    @pl.when(jnp.logical_and(i == 0, j == 0))
    def _():
        h0_scr[...] = _modulated(x_ref, mod_ref, g_ref, k0, gi)

    r0 = pl.multiple_of(jnp.minimum(j * chunk, tm - chunk), BF16_ROWS)

    def step(h_cur, h_next):
        h_next[pl.ds(r0, chunk), :] = _modulated(xn_ref.at[pl.ds(r0, chunk), :], modn_ref, g_ref, k0, gi)
        body(h_cur[...])

    @pl.when(i % 2 == 0)
    def _():
        step(h0_scr, h1_scr)

    @pl.when(i % 2 == 1)
    def _():
        step(h1_scr, h0_scr)


def _tile_input_specs(tm, d, rows_per_mod, g, n_tiles, n_steps, x_every_tile=True):
    chunk = -(-tm // (n_steps * BF16_ROWS)) * BF16_ROWS
    assert chunk <= tm and tm % BF16_ROWS == 0
    nxt = lambda i: jnp.minimum(i + 1, n_tiles - 1)
    x_spec = (pl.BlockSpec((tm, d), lambda i, j: (i, 0)) if x_every_tile else
              pl.BlockSpec((tm, d), lambda i, j: (0, 0), pipeline_mode=pl.Buffered(1)))
    specs = [
        x_spec,
        pl.BlockSpec((tm, d), lambda i, j: (nxt(i), 0)),
        pl.BlockSpec((1, N_MOD, d), lambda i, j: ((i * tm) // rows_per_mod, 0, 0)),
        pl.BlockSpec((1, N_MOD, d), lambda i, j: ((nxt(i) * tm) // rows_per_mod, 0, 0)),
        pl.BlockSpec(g.shape, lambda i, j: (0, 0)),
    ]
    return specs, [pltpu.VMEM((tm, d), BF16), pltpu.VMEM((tm, d), BF16)], chunk


def _ffn_kernel(x_ref, xn_ref, mod_ref, modn_ref, g_ref, wg_ref, wu_ref, wd_ref, o_ref,
                h0_scr, h1_scr, acc_scr, *, k0, gi, chunk):
    j = pl.program_id(1)

    @pl.when(j == 0)
    def _():
        acc_scr[...] = jnp.zeros_like(acc_scr)

    def body(h):
        gate = _dot(h, wg_ref[...])
        up = _dot(h, wu_ref[...])
        a = (_silu(gate) * up).astype(BF16)
        acc_scr[...] += _dot(a, wd_ref[...])

    _with_tile_input((x_ref, xn_ref, mod_ref, modn_ref, g_ref, h0_scr, h1_scr), k0, gi, chunk, body)

    @pl.when(j == pl.num_programs(1) - 1)
    def _():
        m = mod_ref[0]
        y = _rms(acc_scr[...], g_ref[gi + 1:gi + 2, :])
        o_ref[...] = x_ref[...] + (0.5 * m[k0 + 2:k0 + 3, :]) * y


def _ffn(x, mod, rows_per_mod, g, wg, wu, wd, *, layer, which, tm=512, tf=512):
    m_rows, d = x.shape
    f = wg.shape[-1]
    tm = min(tm, rows_per_mod)
    k0, gi = (0, 0) if which == 0 else (6, 4)
    n_tiles, n_steps = m_rows // tm, f // tf
    tile_specs, h_scratch, chunk = _tile_input_specs(tm, d, rows_per_mod, g, n_tiles, n_steps)
    return pl.pallas_call(
        functools.partial(_ffn_kernel, k0=k0, gi=gi, chunk=chunk),
        grid=(n_tiles, n_steps),
        in_specs=tile_specs + [
            pl.BlockSpec((None, None, d, tf), lambda i, j: (layer, which, 0, j)),
            pl.BlockSpec((None, None, d, tf), lambda i, j: (layer, which, 0, j)),
            pl.BlockSpec((None, None, tf, d), lambda i, j: (layer, which, j, 0)),
        ],
        out_specs=pl.BlockSpec((tm, d), lambda i, j: (i, 0)),
        out_shape=jax.ShapeDtypeStruct((m_rows, d), F32),
        scratch_shapes=h_scratch + [pltpu.VMEM((tm, d), F32)],
        compiler_params=_cparams("arbitrary", "arbitrary"),
    )(x, x, mod, mod, g, wg, wu, wd)


def _proj_plain_kernel(x_ref, xn_ref, mod_ref, modn_ref, g_ref, w_ref, o_ref, h0_scr, h1_scr,
                       *, k0, gi, chunk):
    def body(h):
        o_ref[...] = _dot(h, w_ref[...]).astype(o_ref.dtype)

    _with_tile_input((x_ref, xn_ref, mod_ref, modn_ref, g_ref, h0_scr, h1_scr), k0, gi, chunk, body)


def _proj_glu_kernel(x_ref, xn_ref, mod_ref, modn_ref, g_ref, wa_ref, wg_ref, ba_ref, bg_ref, o_ref,
                     h0_scr, h1_scr, *, k0, gi, chunk):
    def body(h):
        for c0 in range(0, o_ref.shape[1], GLU_COL_CHUNK):
            cols = slice(c0, c0 + GLU_COL_CHUNK)
            a = _dot(h, wa_ref[:, cols]) + ba_ref[:, cols]
            gate = _dot(h, wg_ref[:, cols]) + bg_ref[:, cols]
            o_ref[:, cols] = a * jax.nn.sigmoid(gate)

    _with_tile_input((x_ref, xn_ref, mod_ref, modn_ref, g_ref, h0_scr, h1_scr), k0, gi, chunk, body)


def _rope_groups(acc, cos_ref, sin_ref, scale, o_ref):
    n_groups = acc.shape[1] // LANE
    for gidx in range(n_groups):
        xg = acc[:, gidx * LANE:(gidx + 1) * LANE]
        t = (gidx % 2) * LANE
        rot = pltpu.roll(xg, LANE // 2, axis=1)
        o_ref[:, gidx * LANE:(gidx + 1) * LANE] = (
            xg * cos_ref[:, t:t + LANE] + rot * sin_ref[:, t:t + LANE]) * scale


def _proj_gla_kernel(x_ref, xn_ref, mod_ref, modn_ref, g_ref, w_ref, b_ref, cos_ref, sin_ref, o_ref,
                     h0_scr, h1_scr, *, k0, gi, chunk, q_scale, n_qk, n_v, rope):
    j = pl.program_id(1)

    def body(h):
        acc = _dot(h, w_ref[...])

        @pl.when(j < 2 * n_qk)
        def _():
            scale = jnp.where(j < n_qk, q_scale, 1.0).astype(F32)
            if rope:
                _rope_groups(acc, cos_ref, sin_ref, scale, o_ref)
            else:
                o_ref[...] = acc * scale

        @pl.when(jnp.logical_and(j >= 2 * n_qk, j < 2 * n_qk + n_v))
        def _():
            o_ref[...] = acc

        @pl.when(j >= 2 * n_qk + n_v)
        def _():
            o_ref[...] = _silu(acc + b_ref[...])

    _with_tile_input((x_ref, xn_ref, mod_ref, modn_ref, g_ref, h0_scr, h1_scr), k0, gi, chunk, body)


def _gla_gate_kernel(x_ref, mod_ref, g_ref, a1_ref, a2_ref, ba_ref, o_ref, *, k0, gi):
    h = _modulated(x_ref, mod_ref, g_ref, k0, gi)
    low = _dot(h, a1_ref[...]).astype(BF16)
    tm = x_ref.shape[0]
    row = lax.broadcasted_iota(jnp.int32, (tm, tm), 0)
    col = lax.broadcasted_iota(jnp.int32, (tm, tm), 1)
    same_chunk = row // GLA_CHUNK == col // GLA_CHUNK
    for d in range(2):
        z = _dot(low, a2_ref[d]) + ba_ref[d]
        log_sig = jnp.minimum(z, 0.0) - jnp.log1p(jnp.exp(-jnp.abs(z)))
        la = log_sig * (1.0 / GLA_GATE_NORMALIZER)
        ordered = (col <= row) if d == 0 else (col >= row)
        tri = jnp.where(jnp.logical_and(same_chunk, ordered), 1.0, 0.0).astype(BF16)
        la_hi = la.astype(BF16)
        rest = la - la_hi.astype(F32)
        la_mid = rest.astype(BF16)
        la_lo = (rest - la_mid.astype(F32)).astype(BF16)
        o_ref[d] = _dot(tri, la_hi) + _dot(tri, la_mid) + _dot(tri, la_lo)


def _proj_call(kernel_fn, extra_specs, extra_args, x, mod, rows_per_mod, g, n, tn, out_dtype, tm, **static):
    m_rows, d = x.shape
    tm = min(tm, rows_per_mod)
    n_tiles, n_steps = m_rows // tm, n // tn
    tile_specs, h_scratch, chunk = _tile_input_specs(tm, d, rows_per_mod, g, n_tiles, n_steps,
                                                     x_every_tile=False)
    return pl.pallas_call(
        functools.partial(kernel_fn, chunk=chunk, **static),
        grid=(n_tiles, n_steps),
        in_specs=tile_specs + extra_specs(tm),
        out_specs=pl.BlockSpec((tm, tn), lambda i, j: (i, j)),
        out_shape=jax.ShapeDtypeStruct((m_rows, n), out_dtype),
        scratch_shapes=h_scratch,
        compiler_params=_cparams("arbitrary", "arbitrary"),
    )(x, x, mod, mod, g, *extra_args)


def _proj_plain(x, mod, rows_per_mod, g, w, *, k0, gi, out_dtype, tm=1024, tn=512):
    d, n = w.shape
    specs = lambda tm: [pl.BlockSpec((d, tn), lambda i, j: (0, j))]
    return _proj_call(_proj_plain_kernel, specs, (w,), x, mod, rows_per_mod, g, n, tn, out_dtype, tm,
                      k0=k0, gi=gi)


def _proj_glu(x, mod, rows_per_mod, g, w, b, *, k0, gi, tm=512, tn=2048):
    d = w.shape[0]
    n = w.shape[1] // 2
    nb = n // tn
    assert tn % GLU_COL_CHUNK == 0
    b2 = b.reshape(1, 2 * n)
    mode = dict(pipeline_mode=pl.Buffered(1)) if nb == 1 else {}
    specs = lambda tm: [
        pl.BlockSpec((d, tn), lambda i, j: (0, j), **mode),
        pl.BlockSpec((d, tn), lambda i, j: (0, j + nb), **mode),
        pl.BlockSpec((1, tn), lambda i, j: (0, j), **mode),
        pl.BlockSpec((1, tn), lambda i, j: (0, j + nb), **mode),
    ]
    return _proj_call(_proj_glu_kernel, specs, (w, w, b2, b2), x, mod, rows_per_mod, g, n, tn, F32, tm,
                      k0=k0, gi=gi)


def _proj_gla(x, mod, rows_per_mod, g, w_cat, b_r, cos, sin, *, k0, gi, dk_total, dv_total, rope,
              tm=1024, tn=512):
    d, n = w_cat.shape
    n_qk, n_v = dk_total // tn, dv_total // tn
    dk = dk_total // GLA_HEADS

    def specs(tm):
        seq_blocks = cos.shape[0] // tm
        tab_spec = pl.BlockSpec((tm, cos.shape[1]), lambda i, j: (i % seq_blocks, 0))
        return [
            pl.BlockSpec((d, tn), lambda i, j: (0, j)),
            pl.BlockSpec((1, tn), lambda i, j: (0, jnp.maximum(j - (2 * n_qk + n_v), 0))),
            tab_spec, tab_spec,
        ]

    return _proj_call(_proj_gla_kernel, specs, (w_cat, b_r.reshape(1, -1), cos, sin), x, mod, rows_per_mod, g,
                      n, tn, F32, tm, k0=k0, gi=gi, q_scale=dk ** -0.5, n_qk=n_qk, n_v=n_v, rope=rope)


def _gla_gate(x, mod, rows_per_mod, g, a1_cat, a2_pad, b_a, *, k0, gi, tm=256):
    m_rows, d = x.shape
    n = a2_pad.shape[2]
    tm = min(tm, rows_per_mod)
    return pl.pallas_call(
        functools.partial(_gla_gate_kernel, k0=k0, gi=gi),
        grid=(m_rows // tm,),
        in_specs=[
            pl.BlockSpec((tm, d), lambda i: (i, 0)),
            pl.BlockSpec((1, N_MOD, d), lambda i: ((i * tm) // rows_per_mod, 0, 0)),
            pl.BlockSpec(g.shape, lambda i: (0, 0)),
            pl.BlockSpec(a1_cat.shape, lambda i: (0, 0)),
            pl.BlockSpec(a2_pad.shape, lambda i: (0, 0, 0)),
            pl.BlockSpec((2, 1, n), lambda i: (0, 0, 0)),
        ],
        out_specs=pl.BlockSpec((2, tm, n), lambda i: (0, i, 0)),
        out_shape=jax.ShapeDtypeStruct((2, m_rows, n), F32),
        compiler_params=_cparams("parallel"),
    )(x, mod, g, a1_cat, a2_pad, b_a.reshape(2, 1, n))


def _gla_scan_kernel(q_ref, k_ref, v_ref, cum_ref, s0_ref, o_ref, sf_ref, s_scr, *, n_sub, dk, dv):
    d = pl.program_id(1)
    c = pl.program_id(2)
    fwd = d == 0

    @pl.when(c == 0)
    def _():
        s_scr[...] = s0_ref[0, 0]

    ch = GLA_CHUNK
    row = lax.broadcasted_iota(jnp.int32, (ch, ch), 0)
    col = lax.broadcasted_iota(jnp.int32, (ch, ch), 1)
    tri = (row - col) * (1 - 2 * d) >= 0

    def sub(i, carry):
        r0 = pl.multiple_of(jnp.where(fwd, i, n_sub - 1 - i) * ch, ch)
        for h in range(GLA_HEADS):
            qc = q_ref[pl.ds(r0, ch), h * dk:(h + 1) * dk]
            kc = k_ref[pl.ds(r0, ch), h * dk:(h + 1) * dk]
            vc = v_ref[pl.ds(r0, ch), h * dv:(h + 1) * dv].astype(BF16)
            cum = cum_ref[0, pl.ds(r0, ch), h * dk:(h + 1) * dk]
            total = jnp.where(fwd, cum[ch - 1:ch, :], cum[0:1, :])
            qe = (qc * jnp.exp(cum)).astype(BF16)
            ke = (kc * jnp.exp(-cum)).astype(BF16)
            kd = (kc * jnp.exp(total - cum)).astype(BF16)
            scores = jnp.where(tri, _dot_nt(qe, ke), 0.0).astype(BF16)
            s_old = s_scr[h]
            o_ref[0, pl.ds(r0, ch), h * dv:(h + 1) * dv] = _dot(qe, s_old.astype(BF16)) + _dot(scores, vc)
            decay_t = jnp.transpose(jnp.broadcast_to(jnp.exp(total), (LANE, dk)))
            decay = jnp.concatenate([decay_t] * (dv // LANE), axis=1)
            s_scr[h] = decay * s_old + _dot_tn(kd, vc)
        return carry

    lax.fori_loop(0, n_sub, sub, 0, unroll=4)

    @pl.when(c == pl.num_programs(2) - 1)
    def _():
        sf_ref[0, 0] = s_scr[...]


def _gla_scan(p, la, s0, *, batch, seq, dk, dv, n_sub=8):
    h = GLA_HEADS
    n_sub = min(n_sub, seq // GLA_CHUNK)
    rb = n_sub * GLA_CHUNK
    nblk = seq // rb
    kq, kv = h * dk, h * dv
    blk = lambda b, d, c: b * nblk + c + d * (nblk - 1 - 2 * c)
    o, s_fin = pl.pallas_call(
        functools.partial(_gla_scan_kernel, n_sub=n_sub, dk=dk, dv=dv),
        grid=(batch, 2, nblk),
        in_specs=[
            pl.BlockSpec((rb, kq), lambda b, d, c: (blk(b, d, c), 0)),
            pl.BlockSpec((rb, kq), lambda b, d, c: (blk(b, d, c), 1)),
            pl.BlockSpec((rb, kv), lambda b, d, c: (blk(b, d, c), (2 * kq) // kv)),
            pl.BlockSpec((1, rb, kq), lambda b, d, c: (d, blk(b, d, c), 0)),
            pl.BlockSpec((1, 1, h, dk, dv), lambda b, d, c: (b, d, 0, 0, 0)),
        ],
        out_specs=[
            pl.BlockSpec((1, rb, kv), lambda b, d, c: (d, blk(b, d, c), 0)),
            pl.BlockSpec((1, 1, h, dk, dv), lambda b, d, c: (b, d, 0, 0, 0)),
        ],
        out_shape=[
            jax.ShapeDtypeStruct((2, batch * seq, kv), F32),
            jax.ShapeDtypeStruct((batch, 2, h, dk, dv), F32),
        ],
        scratch_shapes=[pltpu.VMEM((h, dk, dv), F32)],
        compiler_params=_cparams("parallel", "parallel", "arbitrary"),
    )(p, p, p, la, s0)
    return o, s_fin


def _na_kernel(q_ref, k_ref, v_ref, kc_ref, vc_ref, bias_ref, o_ref, *, rows, scale, plan):
    kc = kc_ref[...]
    vc = vc_ref[...]
    w = GRID_W
    n_q = NA_BLOCK_ROWS * w
    n_k = (NA_BLOCK_ROWS + NA_WIN_R) * w
    common, exceptions = plan

    def block(b, carry):
        r0 = b * NA_BLOCK_ROWS
        ws = jnp.clip(r0 - NA_WIN_R // 2, 0, rows - (NA_BLOCK_ROWS + NA_WIN_R))
        pid = common
        for blk, p in exceptions:
            pid = jnp.where(b == blk, p, pid)
        q0 = pl.multiple_of(r0 * w, w)
        k0 = pl.multiple_of(ws * w, w)
        q = q_ref[pl.ds(q0, n_q), :]
        kw = k_ref[pl.ds(k0, n_k), :]
        vw = v_ref[pl.ds(k0, n_k), :]
        s_lat = _dot_nt(q, kw) * scale + bias_ref[0, pid]
        s_ctx = _dot_nt(q, kc) * scale
        m = jnp.maximum(jnp.max(s_lat, axis=-1, keepdims=True), jnp.max(s_ctx, axis=-1, keepdims=True))
        p_lat = jnp.exp(s_lat - m)
        p_ctx = jnp.exp(s_ctx - m)
        denom = jnp.sum(p_lat, axis=-1, keepdims=True) + jnp.sum(p_ctx, axis=-1, keepdims=True)
        acc = _dot(p_lat.astype(BF16), vw) + _dot(p_ctx.astype(BF16), vc)
        o_ref[pl.ds(q0, n_q), :] = (acc / denom).astype(o_ref.dtype)
        return carry

    lax.fori_loop(0, rows // NA_BLOCK_ROWS, block, 0, unroll=NA_BLOCK_UNROLL)


def _na_plan(rows):
    kw_rows = NA_BLOCK_ROWS + NA_WIN_R
    patterns, ids = [], []
    for b in range(rows // NA_BLOCK_ROWS):
        r0 = b * NA_BLOCK_ROWS
        ws = min(max(r0 - NA_WIN_R // 2, 0), rows - kw_rows)
        key = []
        for rho in range(NA_BLOCK_ROWS):
            r = r0 + rho
            rs = min(max(r - NA_WIN_R // 2, 0), rows - NA_WIN_R)
            assert ws <= rs and rs + NA_WIN_R <= ws + kw_rows
            key.append((rs - ws, ws - r + NA_WIN_R - 1))
        key = tuple(key)
        if key not in patterns:
            patterns.append(key)
        ids.append(patterns.index(key))
    common = max(set(ids), key=ids.count)
    exceptions = tuple((b, p) for b, p in enumerate(ids) if p != common)
    dr = [[[min(max(base + om, 0), 2 * NA_WIN_R - 2) for om in range(kw_rows)] for (_, base) in pat]
          for pat in patterns]
    valid = [[[first <= om < first + NA_WIN_R for om in range(kw_rows)] for (first, _) in pat]
             for pat in patterns]
    return dr, valid, (common, exceptions)


def _na_bias_table(rpb, rows):
    w = GRID_W
    dr, valid_rows, plan = _na_plan(rows)
    col = jnp.arange(w)
    cs = jnp.clip(col - NA_WIN_C // 2, 0, w - NA_WIN_C)
    valid_cols = (col[None, :] >= cs[:, None]) & (col[None, :] < cs[:, None] + NA_WIN_C)
    rel = jnp.clip(col[None, :] - col[:, None] + (NA_WIN_C - 1), 0, 2 * NA_WIN_C - 2)
    cb = jnp.where(valid_cols[None, None], rpb[:, :, rel], MASK_BIAS).astype(F32)
    masked = jnp.full_like(cb[:, 0], MASK_BIAS)
    strips = [jnp.concatenate([cb[:, k] if ok else masked for k, ok in zip(dr_row, ok_row)], axis=-1)
              for dr_pat, ok_pat in zip(dr, valid_rows) for dr_row, ok_row in zip(dr_pat, ok_pat)]
    t = jnp.stack(strips, axis=1)
    t = t.reshape(rpb.shape[0], len(dr), NA_BLOCK_ROWS * w, (NA_BLOCK_ROWS + NA_WIN_R) * w)
    return t, plan


def _na_attention(qkv, qkv_ctx, bias, plan, *, batch, seq, ctx_len, d):
    hd = d // NA_HEADS
    rows = seq // GRID_W
    assert rows >= NA_BLOCK_ROWS + NA_WIN_R and rows % NA_BLOCK_ROWS == 0 and hd == LANE
    nh = NA_HEADS
    return pl.pallas_call(
        functools.partial(_na_kernel, rows=rows, scale=hd ** -0.5, plan=plan),
        grid=(batch, nh),
        in_specs=[
            pl.BlockSpec((seq, hd), lambda b, h: (b, h)),
            pl.BlockSpec((seq, hd), lambda b, h: (b, nh + h)),
            pl.BlockSpec((seq, hd), lambda b, h: (b, 2 * nh + h)),
            pl.BlockSpec((ctx_len, hd), lambda b, h: (b, nh + h)),
            pl.BlockSpec((ctx_len, hd), lambda b, h: (b, 2 * nh + h)),
            pl.BlockSpec((1,) + bias.shape[1:], lambda b, h: (h, 0, 0, 0)),
        ],
        out_specs=pl.BlockSpec((seq, hd), lambda b, h: (b, h)),
        out_shape=jax.ShapeDtypeStruct((batch * seq, d), BF16),
        compiler_params=_cparams("parallel", "parallel"),
    )(qkv, qkv, qkv, qkv_ctx, qkv_ctx, bias)


def _finish(a, w_ref, b_ref, x_ref, mod_ref, g_ref, o_ref):
    y = _dot(a, w_ref[...])
    if b_ref is not None:
        y = y + b_ref[...]
    m = mod_ref[0]
    o_ref[...] = x_ref[...] + m[5:6, :] * _rms(y, g_ref[3:4, :])


def _out_na_kernel(a_ref, w_ref, x_ref, mod_ref, g_ref, o_ref):
    _finish(a_ref[...], w_ref, None, x_ref, mod_ref, g_ref, o_ref)


def _out_gla_kernel(of_ref, ob_ref, r_ref, ng_ref, w_ref, x_ref, mod_ref, g_ref, o_ref, a_scr, *, dv):
    ng = ng_ref[...]
    for h in range(GLA_HEADS):
        sl = slice(h * dv, (h + 1) * dv)
        o = of_ref[0, :, sl] + ob_ref[0, :, sl]
        a_scr[:, sl] = (_rms(o, ng) * r_ref[:, sl]).astype(BF16)
    _finish(a_scr[...], w_ref, None, x_ref, mod_ref, g_ref, o_ref)


def _out_conv_kernel(prev_ref, cur_ref, next_ref, wdw_ref, bdw_ref, lng_ref, lnb_ref, w_ref, b_ref,
                     x_ref, mod_ref, g_ref, o_ref, ext_scr, u_scr, *, seq, width, rb, ls):
    tm = cur_ref.shape[0]
    pos = (pl.program_id(0) * tm) % seq
    halo = CONV_HALO
    ext_scr[0:halo, :] = jnp.where(pos > 0, prev_ref[...], 0.0)
    ext_scr[halo:halo + tm, :] = cur_ref[...]
    ext_scr[halo + tm:, :] = jnp.where(pos + tm < seq, next_ref[...], 0.0)
    off = halo - width // 2
    d = cur_ref.shape[1]
    n_q = (off + width - 1) // SUBLANE + 1

    def rows(blk, carry):
        r0 = pl.multiple_of(blk * rb, rb)
        for c0 in range(0, d, ls):
            v = ext_scr[pl.ds(r0, rb + 2 * halo), c0:c0 + ls]
            acc = jnp.broadcast_to(bdw_ref[:, c0:c0 + ls], (rb, ls))
            for s in range(SUBLANE):
                vs = v[s:s + rb + SUBLANE * (n_q - 1)]
                for q in range(n_q):
                    k = SUBLANE * q + s - off
                    if 0 <= k < width:
                        wk = jnp.concatenate([wdw_ref[k, :, c0:c0 + ls]] * (rb // SUBLANE), axis=0)
                        acc = acc + wk * vs[SUBLANE * q:SUBLANE * q + rb]
            u_scr[pl.ds(r0, rb), c0:c0 + ls] = acc
        return carry

    lax.fori_loop(0, tm // rb, rows, 0)
    u = u_scr[...]
    mu = jnp.mean(u, axis=-1, keepdims=True)
    uc = u - mu
    var = jnp.mean(uc * uc, axis=-1, keepdims=True)
    y = uc * lax.rsqrt(var + NORM_EPS) * lng_ref[...] + lnb_ref[...]
    _finish(_silu(y).astype(BF16), w_ref, b_ref, x_ref, mod_ref, g_ref, o_ref)


def _out_tail_specs(tm, d, rows_per_mod, g):
    return [
        pl.BlockSpec((d, d), lambda i: (0, 0)),
        pl.BlockSpec((tm, d), lambda i: (i, 0)),
        pl.BlockSpec((1, N_MOD, d), lambda i: ((i * tm) // rows_per_mod, 0, 0)),
        pl.BlockSpec(g.shape, lambda i: (0, 0)),
    ]


def _out_call(kernel, head_specs, head_args, w, x, mod, rows_per_mod, g, tm, scratch=()):
    m_rows, d = x.shape
    tail = _out_tail_specs(tm, d, rows_per_mod, g)
    return pl.pallas_call(
        kernel,
        grid=(m_rows // tm,),
        in_specs=head_specs + tail,
        out_specs=pl.BlockSpec((tm, d), lambda i: (i, 0)),
        out_shape=jax.ShapeDtypeStruct((m_rows, d), F32),
        scratch_shapes=list(scratch),
        compiler_params=_cparams("parallel"),
    )(*head_args, w, x, mod, g)


def _out_na(a, w, x, mod, rows_per_mod, g, tm=512):
    d = x.shape[1]
    tm = min(tm, rows_per_mod)
    return _out_call(_out_na_kernel, [pl.BlockSpec((tm, d), lambda i: (i, 0))], [a],
                     w, x, mod, rows_per_mod, g, tm)


def _out_gla(o, p, norm_g, w, x, mod, rows_per_mod, g, *, dv, tm=256):
    d = x.shape[1]
    tm = min(tm, rows_per_mod)
    r_blk = (p.shape[1] - d) // d
    specs = [
        pl.BlockSpec((1, tm, d), lambda i: (0, i, 0)),
        pl.BlockSpec((1, tm, d), lambda i: (1, i, 0)),
        pl.BlockSpec((tm, d), lambda i: (i, r_blk)),
        pl.BlockSpec((1, dv), lambda i: (0, 0)),
    ]
    return _out_call(functools.partial(_out_gla_kernel, dv=dv), specs, [o, o, p, norm_g.reshape(1, dv)],
                     w, x, mod, rows_per_mod, g, tm, scratch=[pltpu.VMEM((tm, d), BF16)])


def _out_conv(u, w_dw, b_dw, ln_g, ln_b, w, b, x, mod, rows_per_mod, g, tm=256,
              rb=CONV_ROW_BLOCK, ls=CONV_LANE_BLOCK):
    m_rows, d = x.shape
    tm = min(tm, rows_per_mod)
    width = w_dw.shape[0]
    assert width // 2 < CONV_HALO and tm % CONV_HALO == 0 and tm % rb == 0 and d % ls == 0
    hb = tm // CONV_HALO
    last = m_rows // CONV_HALO - 1
    row = lambda v: v.reshape(1, d)
    specs = [
        pl.BlockSpec((CONV_HALO, d), lambda i: (jnp.maximum(i * hb - 1, 0), 0)),
        pl.BlockSpec((tm, d), lambda i: (i, 0)),
        pl.BlockSpec((CONV_HALO, d), lambda i: (jnp.minimum((i + 1) * hb, last), 0)),
        pl.BlockSpec((width, SUBLANE, d), lambda i: (0, 0, 0)),
    ] + [pl.BlockSpec((1, d), lambda i: (0, 0))] * 3
    kernel = functools.partial(_out_conv_kernel, seq=rows_per_mod, width=width, rb=rb, ls=ls)
    tail = _out_tail_specs(tm, d, rows_per_mod, g)
    return pl.pallas_call(
        kernel,
        grid=(m_rows // tm,),
        in_specs=specs + [tail[0], pl.BlockSpec((1, d), lambda i: (0, 0))] + tail[1:],
        out_specs=pl.BlockSpec((tm, d), lambda i: (i, 0)),
        out_shape=jax.ShapeDtypeStruct((m_rows, d), F32),
        scratch_shapes=[pltpu.VMEM((tm + 2 * CONV_HALO, d), F32), pltpu.VMEM((tm, d), F32)],
        compiler_params=_cparams("parallel"),
    )(u, u, u, jnp.broadcast_to(w_dw[:, None, :], (width, SUBLANE, d)),
      row(b_dw), row(ln_g), row(ln_b), w, row(b), x, mod, g)


def _rope_tables(seq, dk):
    n = dk // 2
    inv = ROPE_THETA ** (-jnp.arange(0, n, 2, dtype=F32) / n)
    t = jnp.arange(seq)

    def half(pos):
        ang = pos.astype(F32)[:, None] * inv
        c, s = jnp.cos(ang), jnp.sin(ang)
        return jnp.concatenate([c, c], axis=-1), jnp.concatenate([-s, s], axis=-1)

    cr, sr = half(t // GRID_W)
    cc, sc = half(t % GRID_W)
    return jnp.concatenate([cr, cc], axis=-1), jnp.concatenate([sr, sc], axis=-1)


def kernel(x, c, ctx, c_ctx, ada_w, ada_b, norm_g, ffn_w_gate, ffn_w_up, ffn_w_down, conv_w_in, conv_b_in, conv_w_dw, conv_b_dw, conv_ln_g, conv_ln_b, conv_w_out, conv_b_out, gla_w_q, gla_w_k, gla_w_v, gla_w_r, gla_b_r, gla_w_a1, gla_w_a2, gla_b_a, gla_norm_g, gla_w_o, na_w_qkv, na_rpb, na_w_o):
    batch, seq, d = x.shape
    ctx_len = ctx.shape[1]
    depth = ada_w.shape[0]
    dk_total = gla_w_q.shape[-1]
    dv_total = gla_w_v.shape[-1]
    dk, dv = dk_total // GLA_HEADS, dv_total // GLA_HEADS

    pad = (-(batch + 1)) % 8
    cs = jnp.concatenate([c, c_ctx[None, :], jnp.zeros((pad, d), F32)], axis=0)
    mods = _ada(cs, ada_w, ada_b).reshape(depth, batch + 1 + pad, N_MOD, d)

    wg_all, wu_all, wd_all = (w.astype(BF16) for w in (ffn_w_gate, ffn_w_up, ffn_w_down))

    x_lat = x.reshape(batch * seq, d)
    x_ctx = ctx.reshape(batch * ctx_len, d)
    reads_ctx = [(i % N_MIXERS) != 0 for i in range(depth)]

    for i in range(depth):
        kind, j = i % N_MIXERS, i // N_MIXERS
        ctx_out = any(reads_ctx[i + 1:])
        ctx_in = ctx_out or reads_ctx[i]
        mod_lat = mods[i, :batch]
        mod_ctx = jnp.broadcast_to(mods[i, batch], (batch, N_MOD, d))
        g = norm_g[i]
        lat = dict(mod=mod_lat, rows_per_mod=seq, g=g)
        cx = dict(mod=mod_ctx, rows_per_mod=ctx_len, g=g)

        ffn1 = dict(wg=wg_all, wu=wu_all, wd=wd_all, layer=i, which=0)
        ffn2 = dict(wg=wg_all, wu=wu_all, wd=wd_all, layer=i, which=1)

        cx_ffn = dict(mod=mod_ctx[:1], rows_per_mod=batch * ctx_len, g=g)
        x_lat = _ffn(x_lat, **lat, **ffn1)
        if ctx_in:
            x_ctx = _ffn(x_ctx, **cx_ffn, **ffn1)

        x_ctx_mixed = None
        if kind == 0:
            w_in = conv_w_in[j].astype(BF16)
            w_out = conv_w_out[j].astype(BF16)
            tail = (conv_w_dw[j], conv_b_dw[j], conv_ln_g[j], conv_ln_b[j], w_out, conv_b_out[j])
            u = _proj_glu(x_lat, w=w_in, b=conv_b_in[j], k0=3, gi=2, **lat)
            x_lat = _out_conv(u, *tail, x_lat, mod_lat, seq, g)
            if ctx_out:
                uc = _proj_glu(x_ctx, w=w_in, b=conv_b_in[j], k0=3, gi=2, **cx)
                x_ctx_mixed = _out_conv(uc, *tail, x_ctx, mod_ctx, ctx_len, g)
        elif kind == 1:
            w_cat = jnp.concatenate([gla_w_q[j], gla_w_k[j], gla_w_v[j], gla_w_r[j]], axis=1).astype(BF16)
            a1_cat = jnp.concatenate(
                [gla_w_a1[j, 0], gla_w_a1[j, 1], jnp.zeros((d, LANE - 2 * GLA_GATE_RANK), F32)], axis=1).astype(BF16)
            a2_pad = jnp.zeros((2, LANE, dk_total), F32)
            for dd in range(2):
                a2_pad = a2_pad.at[dd, dd * GLA_GATE_RANK:(dd + 1) * GLA_GATE_RANK].set(gla_w_a2[j, dd])
            a2_pad = a2_pad.astype(BF16)
            w_o = gla_w_o[j].astype(BF16)
            cos, sin = _rope_tables(seq, dk)
            proj = dict(w_cat=w_cat, b_r=gla_b_r[j], k0=3, gi=2, dk_total=dk_total, dv_total=dv_total)
            gate = dict(a1_cat=a1_cat, a2_pad=a2_pad, b_a=gla_b_a[j], k0=3, gi=2)

            p_ctx = _proj_gla(x_ctx, cos=cos, sin=sin, rope=False, **cx, **proj)
            la_ctx = _gla_gate(x_ctx, **cx, **gate)
            s0 = jnp.zeros((batch, 2, GLA_HEADS, dk, dv), F32)
            o_ctx, s_ctx = _gla_scan(p_ctx, la_ctx, s0, batch=batch, seq=ctx_len, dk=dk, dv=dv)

            p_lat = _proj_gla(x_lat, cos=cos, sin=sin, rope=True, **lat, **proj)
            la_lat = _gla_gate(x_lat, **lat, **gate)
            o_lat, _ = _gla_scan(p_lat, la_lat, s_ctx, batch=batch, seq=seq, dk=dk, dv=dv)

            x_lat = _out_gla(o_lat, p_lat, gla_norm_g[j], w_o, x_lat, mod_lat, seq, g, dv=dv)
            if ctx_out:
                x_ctx_mixed = _out_gla(o_ctx, p_ctx, gla_norm_g[j], w_o, x_ctx, mod_ctx, ctx_len, g, dv=dv)
        else:
            assert not ctx_out, "context self-attention output is not needed for this depth"
            w_qkv = na_w_qkv[j].astype(BF16)
            qkv = _proj_plain(x_lat, w=w_qkv, k0=3, gi=2, out_dtype=BF16, **lat)
            qkv_ctx = _proj_plain(x_ctx, w=w_qkv, k0=3, gi=2, out_dtype=BF16, **cx)
            bias, plan = _na_bias_table(na_rpb[j], seq // GRID_W)
            a = _na_attention(qkv, qkv_ctx, bias, plan, batch=batch, seq=seq, ctx_len=ctx_len, d=d)
            x_lat = _out_na(a, na_w_o[j].astype(BF16), x_lat, mod_lat, seq, g)

        x_lat = _ffn(x_lat, **lat, **ffn2)
        if ctx_out:
            x_ctx = _ffn(x_ctx_mixed, **cx_ffn, **ffn2)

    return x_lat.reshape(batch, seq, d)
```
